```python
import math
import jax
import jax.numpy as jnp
from jax import lax
import numpy as np

D_MODEL = 2048
BATCH = 4
SEQ = 2048
DEPTH = 2

MIX_WIDTH = D_MODEL
N_MIXERS = 4
GROUP_WIDTH = MIX_WIDTH // N_MIXERS
FN_HEADS = 4
FN_HEAD_DIM = GROUP_WIDTH // FN_HEADS
S5_GROUP_CH = 16
S5_GROUPS = GROUP_WIDTH // S5_GROUP_CH
S5_STATE = 64
HEAD_DIM = 128
ATT_HEADS = GROUP_WIDTH // HEAD_DIM
KV_HEADS = ATT_HEADS // 2
Q_BLOCK = 128
GRID_W = 64
ROPE_THETA = 10000.0
AXIS_ROT_DIM = HEAD_DIM // 2
RW_HEAD_DIM = 64
RW_HEADS = GROUP_WIDTH // RW_HEAD_DIM
RW_DECAY_LORA = 32
RW_AICL_LORA = 32
RW_GATE_LORA = 96
RW_GN_EPS = 64e-5
PEER_HEADS = 8
PEER_KEY_DIM = 128
PEER_HALF = PEER_KEY_DIM // 2
PEER_N_KEYS = 128
PEER_EXPERTS = PEER_N_KEYS * PEER_N_KEYS
PEER_TOPK = 16
PEER_CHUNK = 128
NORM_EPS = 1e-6

COL_SIZES = (GROUP_WIDTH, GROUP_WIDTH, ATT_HEADS * HEAD_DIM, KV_HEADS * HEAD_DIM,
             KV_HEADS * HEAD_DIM, GROUP_WIDTH, GROUP_WIDTH, GROUP_WIDTH, GROUP_WIDTH)
PROJ_WIDTH = sum(COL_SIZES)
SPLIT_POINTS = tuple(sum(COL_SIZES[:i + 1]) for i in range(len(COL_SIZES) - 1))

kernel_name = 'hymba_style_hybrid_encoder_peer'


def rms_norm(x, gain):
    xf = x.astype(jnp.float32)
    y = xf * lax.rsqrt(jnp.mean(xf * xf, axis=-1, keepdims=True) + NORM_EPS)
    return (y * gain.astype(jnp.float32)).astype(x.dtype)


def fourier_mix(u, w_f):
    bn, s, _ = u.shape
    uh = u.astype(jnp.float32).reshape(bn, s, FN_HEADS, FN_HEAD_DIM)
    f = jnp.fft.fft2(uh, axes=(1, 3), norm='ortho').real
    return f.reshape(bn, s, GROUP_WIDTH).astype(u.dtype) @ w_f


def _ssm_combine(e1, e2):
    a1, b1 = e1
    a2, b2 = e2
    return a1 * a2, a2 * b1 + b2


def s5_mix(u, lam_re, lam_im, log_step, b_re, b_im, c_re, c_im, d_skip, w_glu):
    f32 = jnp.float32
    bn, s, _ = u.shape
    uf = u.astype(f32).reshape(bn, s, S5_GROUPS, S5_GROUP_CH)
    uc = uf.astype(jnp.complex64)
    ys = []
    for d in range(2):
        lam = lax.complex(lam_re[d].astype(f32), lam_im[d].astype(f32))
        step = jnp.exp(log_step[d].astype(f32))[:, None]
        lam_bar = jnp.exp(lam * step)
        bmat = lax.complex(b_re[d].astype(f32), b_im[d].astype(f32))
        b_bar = ((lam_bar - 1.0) / lam)[..., None] * bmat
        bu = jnp.einsum('gph,bsgh->bsgp', b_bar, uc)
        a = jnp.broadcast_to(lam_bar, bu.shape)
        _, states = lax.associative_scan(_ssm_combine, (a, bu), reverse=(d == 1), axis=1)
        cmat = lax.complex(c_re[d].astype(f32), c_im[d].astype(f32))
        ys.append(jnp.einsum('ghp,bsgp->bsgh', cmat, states).real)
    y = ys[0] + ys[1] + d_skip.astype(f32) * uf
    y = jax.nn.gelu(y.reshape(bn, s, GROUP_WIDTH)).astype(u.dtype)
    return y * jax.nn.sigmoid(y @ w_glu)


def axial_rope_tables(seq_len):
    rows = seq_len // GRID_W
    row_id = jnp.repeat(jnp.arange(rows), GRID_W).astype(jnp.float32)
    col_id = jnp.tile(jnp.arange(GRID_W), rows).astype(jnp.float32)
    inv = ROPE_THETA ** (-jnp.arange(0, AXIS_ROT_DIM, 2, dtype=jnp.float32) / AXIS_ROT_DIM)
    ang = jnp.concatenate([row_id[:, None] * inv, col_id[:, None] * inv], axis=-1)
    return jnp.cos(ang), jnp.sin(ang)


def apply_rope(x, cos, sin):
    xf = x.astype(jnp.float32)
    half = HEAD_DIM // 2
    x1, x2 = xf[..., :half], xf[..., half:]
    c = cos[None, :, None, :]
    s = sin[None, :, None, :]
    return jnp.concatenate([x1 * c - x2 * s, x1 * s + x2 * c], axis=-1).astype(x.dtype)


def block_attention(q, k, v):
    bn, s, _, _ = q.shape
    rep = ATT_HEADS // KV_HEADS
    nb = s // Q_BLOCK
    qb = q.reshape(bn, nb, Q_BLOCK, KV_HEADS, rep, HEAD_DIM).transpose(1, 0, 2, 3, 4, 5)
    scale = HEAD_DIM ** -0.5

    def one_block(qblk):
        sc = jnp.einsum('bqgrd,bkgd->bgrqk', qblk, k, preferred_element_type=jnp.float32) * scale
        p = jax.nn.softmax(sc, axis=-1).astype(v.dtype)
        return jnp.einsum('bgrqk,bkgd->bqgrd', p, v)

    o = lax.map(one_block, qb)
    return o.transpose(1, 0, 2, 3, 4, 5).reshape(bn, s, ATT_HEADS * HEAD_DIM)


def attention_mix(q, k, v, q_gain, k_gain, cos, sin):
    bn, s, _ = q.shape
    q = q.reshape(bn, s, ATT_HEADS, HEAD_DIM)
    k = k.reshape(bn, s, KV_HEADS, HEAD_DIM)
    v = v.reshape(bn, s, KV_HEADS, HEAD_DIM)
    q = apply_rope(rms_norm(q, q_gain), cos, sin)
    k = apply_rope(rms_norm(k, k_gain), cos, sin)
    return block_attention(q, k, v)


def centred_shift(x, mu):
    prev = jnp.pad(x[:, :-1], ((0, 0), (1, 0), (0, 0)))
    nxt = jnp.pad(x[:, 1:], ((0, 0), (0, 1), (0, 0)))
    return x + (prev - x) * mu[0] + (nxt - x) * mu[1]


def rwkv_scan(r, w, k, v, kk, b, reverse):
    bn, _, h, n = r.shape

    def step(state, inp):
        r_t, w_t, k_t, v_t, kk_t, b_t = inp
        sa = jnp.einsum('bhvk,bhk->bhv', state, -kk_t)
        state = (state * w_t[:, :, None, :] + sa[..., None] * b_t[:, :, None, :]
                 + v_t[..., None] * k_t[:, :, None, :])
        return state, jnp.einsum('bhvk,bhk->bhv', state, r_t)

    s0 = jnp.zeros((bn, h, n, n), jnp.float32)
    xs = tuple(jnp.moveaxis(t, 1, 0) for t in (r, w, k, v, kk, b))
    _, ys = lax.scan(step, s0, xs, reverse=reverse)
    return jnp.moveaxis(ys, 0, 1)


def rwkv7_mix(r, k, v, z, mu, w0, w1, w2, a0, a1, a2, g1, g2, k_k, k_a, r_k, ln_w, ln_b):
    f32 = jnp.float32
    bn, s, c = r.shape
    r = centred_shift(r, mu[0])
    k = centred_shift(k, mu[1])
    v = centred_shift(v, mu[2])
    z = centred_shift(z, mu[3])

    def heads(t):
        return t.astype(f32).reshape(bn, s, RW_HEADS, RW_HEAD_DIM)

    g = jax.nn.sigmoid(z @ g1) @ g2
    kk = heads(k * k_k)
    kk = kk * lax.rsqrt(jnp.sum(kk * kk, axis=-1, keepdims=True) + 1e-12)
    rh = heads(r)
    vh = heads(v)
    ys = []
    bonuses = []
    for d in range(2):
        w_log = -jax.nn.softplus(-(w0[d] + jnp.tanh(z @ w1[d]) @ w2[d])) - 0.5
        decay = heads(jnp.exp(-jnp.exp(w_log.astype(f32))))
        a = jax.nn.sigmoid(a0[d] + (z @ a1[d]) @ a2[d])
        kd = heads(k * (1.0 + (a - 1.0) * k_a))
        ys.append(rwkv_scan(rh, decay, kd, vh, kk, kk * heads(a), reverse=(d == 1)))
        bonuses.append(jnp.sum(rh * kd * r_k.astype(f32), axis=-1, keepdims=True) * vh)
    y = ys[0] + ys[1]
    mean = jnp.mean(y, axis=-1, keepdims=True)
    var = jnp.mean(jnp.square(y - mean), axis=-1, keepdims=True)
    yn = ((y - mean) * lax.rsqrt(var + RW_GN_EPS)).reshape(bn, s, c)
    yn = yn * ln_w.astype(f32) + ln_b.astype(f32)
    out = (yn + (bonuses[0] + bonuses[1]).reshape(bn, s, c)) * g.astype(f32)
    return out.astype(r.dtype)


def peer_ffn(h, w_query, sub_keys, expert_u, expert_v):
    bn, s, d = h.shape
    xt = h.reshape(bn * s // PEER_CHUNK, PEER_CHUNK, d)
    kk2 = PEER_TOPK * PEER_TOPK

    def chunk(xc):
        q = (xc @ w_query).reshape(PEER_CHUNK, PEER_HEADS, 2, PEER_HALF)
        sc = jnp.einsum('chpd,hpkd->chpk', q, sub_keys, preferred_element_type=jnp.float32)
        s1, i1 = lax.top_k(sc[:, :, 0], PEER_TOPK)
        s2, i2 = lax.top_k(sc[:, :, 1], PEER_TOPK)
        cand_s = (s1[..., :, None] + s2[..., None, :]).reshape(PEER_CHUNK, PEER_HEADS, kk2)
        cand_i = (i1[..., :, None] * PEER_N_KEYS + i2[..., None, :]).reshape(PEER_CHUNK, PEER_HEADS, kk2)
        top_s, pos = lax.top_k(cand_s, PEER_TOPK)
        idx = jnp.take_along_axis(cand_i, pos, axis=-1)
        gate = jax.nn.softmax(top_s, axis=-1)
        u = jnp.take(expert_u, idx, axis=0)
        act = jax.nn.gelu(jnp.einsum('chkd,cd->chk', u, xc, preferred_element_type=jnp.float32),
                          approximate=False)
        vsel = jnp.take(expert_v, idx, axis=0)
        return jnp.einsum('chk,chkd->cd', (gate * act).astype(xc.dtype), vsel)

    return lax.map(chunk, xt).reshape(bn, s, d)


def setup_inputs(seed: int = 0) -> dict:
    key = jax.random.key(seed)
    keys = list(jax.random.split(key, 48))
    f32 = jnp.float32
    L = DEPTH
    G, P, GW = S5_GROUPS, S5_STATE, GROUP_WIDTH

    def nrm(shape, scale):
        return scale * jax.random.normal(keys.pop(), shape, f32)

    def gain(shape):
        return 1.0 + nrm(shape, 0.02)

    w0_base = -6.5 + 5.0 * jnp.linspace(0.0, 1.0, GW, dtype=f32) ** 0.9
    return {
        'x': nrm((BATCH, SEQ, D_MODEL), 1.0),
        'norm_mix': gain((L, D_MODEL)),
        'w_in': nrm((L, D_MODEL, PROJ_WIDTH), D_MODEL ** -0.5),
        'fourier_w': nrm((L, GW, GW), GW ** -0.5),
        's5_lam_re': -0.5 + nrm((L, 2, G, P), 0.01),
        's5_lam_im': jnp.pi * jnp.arange(P, dtype=f32) + nrm((L, 2, G, P), 0.01),
        's5_log_step': jax.random.uniform(keys.pop(), (L, 2, G), f32,
                                          minval=math.log(1e-3), maxval=math.log(1e-1)),
        's5_b_re': nrm((L, 2, G, P, S5_GROUP_CH), (2 * S5_GROUP_CH) ** -0.5),
        's5_b_im': nrm((L, 2, G, P, S5_GROUP_CH), (2 * S5_GROUP_CH) ** -0.5),
        's5_c_re': nrm((L, 2, G, S5_GROUP_CH, P), P ** -0.5),
        's5_c_im': nrm((L, 2, G, S5_GROUP_CH, P), P ** -0.5),
        's5_d': nrm((L, G, S5_GROUP_CH), 1.0),
        's5_w_glu': nrm((L, GW, GW), GW ** -0.5),
        'attn_q_norm': gain((L, HEAD_DIM)),
        'attn_k_norm': gain((L, HEAD_DIM)),
        'rwkv_mu': jax.random.uniform(keys.pop(), (L, 4, 2, GW), f32, minval=0.0, maxval=0.5),
        'rwkv_w0': w0_base + nrm((L, 2, GW), 0.1),
        'rwkv_w1': nrm((L, 2, GW, RW_DECAY_LORA), GW ** -0.5),
        'rwkv_w2': nrm((L, 2, RW_DECAY_LORA, GW), 0.1 * RW_DECAY_LORA ** -0.5),
        'rwkv_a0': nrm((L, 2, GW), 0.1),
        'rwkv_a1': nrm((L, 2, GW, RW_AICL_LORA), GW ** -0.5),
        'rwkv_a2': nrm((L, 2, RW_AICL_LORA, GW), RW_AICL_LORA ** -0.5),
        'rwkv_g1': nrm((L, GW, RW_GATE_LORA), GW ** -0.5),
        'rwkv_g2': nrm((L, RW_GATE_LORA, GW), RW_GATE_LORA ** -0.5),
        'rwkv_k_k': 0.85 + nrm((L, GW), 0.02),
        'rwkv_k_a': gain((L, GW)),
        'rwkv_r_k': nrm((L, RW_HEADS, RW_HEAD_DIM), 0.1),
        'rwkv_ln_w': gain((L, GW)),
        'rwkv_ln_b': nrm((L, GW), 0.02),
        'branch_norm': gain((L, MIX_WIDTH)),
        'w_out': nrm((L, MIX_WIDTH, D_MODEL), MIX_WIDTH ** -0.5),
        'norm_ffn': gain((L, D_MODEL)),
        'peer_w_query': nrm((L, D_MODEL, PEER_HEADS * PEER_KEY_DIM), D_MODEL ** -0.5),
        'peer_sub_keys': nrm((L, PEER_HEADS, 2, PEER_N_KEYS, PEER_HALF), PEER_HALF ** -0.5),
        'peer_u': nrm((L, PEER_EXPERTS, D_MODEL), D_MODEL ** -0.5),
        'peer_v': nrm((L, PEER_EXPERTS, D_MODEL), PEER_HEADS ** -0.5),
    }


def reference(x, norm_mix, w_in, fourier_w, s5_lam_re, s5_lam_im, s5_log_step, s5_b_re, s5_b_im,
              s5_c_re, s5_c_im, s5_d, s5_w_glu, attn_q_norm, attn_k_norm, rwkv_mu, rwkv_w0,
              rwkv_w1, rwkv_w2, rwkv_a0, rwkv_a1, rwkv_a2, rwkv_g1, rwkv_g2, rwkv_k_k, rwkv_k_a,
              rwkv_r_k, rwkv_ln_w, rwkv_ln_b, branch_norm, w_out, norm_ffn, peer_w_query,
              peer_sub_keys, peer_u, peer_v):
    bn, s, _ = x.shape
    cos, sin = axial_rope_tables(s)
    for l in range(DEPTH):
        h = rms_norm(x, norm_mix[l])
        proj = h @ w_in[l]
        pa, pb, pq, pk, pv, rr, rk, rv, rz = jnp.split(proj, SPLIT_POINTS, axis=-1)
        ya = fourier_mix(pa, fourier_w[l])
        yb = s5_mix(pb, s5_lam_re[l], s5_lam_im[l], s5_log_step[l], s5_b_re[l], s5_b_im[l],
                    s5_c_re[l], s5_c_im[l], s5_d[l], s5_w_glu[l])
        yc = attention_mix(pq, pk, pv, attn_q_norm[l], attn_k_norm[l], cos, sin)
        yd = rwkv7_mix(rr, rk, rv, rz, rwkv_mu[l], rwkv_w0[l], rwkv_w1[l], rwkv_w2[l],
                       rwkv_a0[l], rwkv_a1[l], rwkv_a2[l], rwkv_g1[l], rwkv_g2[l],
                       rwkv_k_k[l], rwkv_k_a[l], rwkv_r_k[l], rwkv_ln_w[l], rwkv_ln_b[l])
        ycat = jnp.stack([ya, yb, yc.astype(ya.dtype), yd.astype(ya.dtype)], axis=2)
        ycat = rms_norm(ycat, branch_norm[l].reshape(N_MIXERS, GROUP_WIDTH))
        x = x + ycat.reshape(bn, s, MIX_WIDTH) @ w_out[l]
        x = x + peer_ffn(rms_norm(x, norm_ffn[l]), peer_w_query[l], peer_sub_keys[l],
                         peer_u[l], peer_v[l])
    return x
```

```python
import functools
import math

import jax
import jax.numpy as jnp
from jax import lax
from jax.experimental import pallas as pl
from jax.experimental.pallas import tpu as pltpu

F32 = jnp.float32
BF16 = jnp.bfloat16

D_MODEL = 2048
GROUP_WIDTH = 512
N_MIXERS = 4
FN_HEADS = 4
FN_HEAD_DIM = 128
S5_GROUP_CH = 16
S5_GROUPS = 32
S5_STATE = 64
S5_NSTATE = S5_GROUPS * S5_STATE
HEAD_DIM = 128
ATT_HEADS = 4
KV_HEADS = 2
GRID_W = 64
ROPE_THETA = 10000.0
AXIS_ROT_DIM = 64
RW_HEAD_DIM = 64
RW_HEADS = 8
RW_GN_EPS = 64e-5
LORA_PAD = 128
PEER_HEADS = 8
PEER_HALF = 64
PEER_N_KEYS = 128
PEER_TOPK = 16
NORM_EPS = 1e-6
PROJ_WIDTH = 4096

LANES = 128
VMEM_LIMIT = 56 * 1024 * 1024


def _cparams(sem):
    return pltpu.CompilerParams(dimension_semantics=sem, vmem_limit_bytes=VMEM_LIMIT)


def _rms(x, gain):
    return x * lax.rsqrt(jnp.mean(x * x, axis=-1, keepdims=True) + NORM_EPS) * gain


def _split2(x):
    hi = x.astype(BF16)
    lo = (x - hi.astype(F32)).astype(BF16)
    return hi, lo


def _norm_matmul_kernel(x_ref, g_ref, w_ref, o_ref, xn_out_ref, xn_ref):
    @pl.when(pl.program_id(1) == 0)
    def _():
        xn = _rms(x_ref[...], g_ref[...]).astype(BF16)
        xn_ref[...] = xn
        xn_out_ref[...] = xn

    o_ref[...] = jnp.dot(xn_ref[...], w_ref[...], preferred_element_type=F32)


def _norm_matmul(x, gain, w, *, tm=512, tn=1024):
    m, k = x.shape
    n = w.shape[1]
    return pl.pallas_call(
        _norm_matmul_kernel,
        grid=(m // tm, n // tn),
        in_specs=[
            pl.BlockSpec((tm, k), lambda i, j: (i, 0)),
            pl.BlockSpec((1, k), lambda i, j: (0, 0)),
            pl.BlockSpec((k, tn), lambda i, j: (0, j)),
        ],
        out_specs=[
            pl.BlockSpec((tm, tn), lambda i, j: (i, j)),
            pl.BlockSpec((tm, k), lambda i, j: (i, 0)),
        ],
        out_shape=[
            jax.ShapeDtypeStruct((m, n), F32),
            jax.ShapeDtypeStruct((m, k), BF16),
        ],
        scratch_shapes=[pltpu.VMEM((tm, k), BF16)],
        compiler_params=_cparams(("parallel", "arbitrary")),
        name="norm_matmul",
    )(x, gain.reshape(1, k), w)


def _fnet_kernel(pa_ref, dftc_ref, dfts_ref, wf_ref, g_ref, o_ref, z_ref, *, seq):
    @pl.when(pl.program_id(1) == 0)
    def _():
        for h in range(FN_HEADS):
            sl = slice(h * FN_HEAD_DIM, (h + 1) * FN_HEAD_DIM)
            xh = pa_ref[0, :, sl].astype(BF16)
            t = jnp.dot(xh, dftc_ref[...], preferred_element_type=F32)
            z_ref[0:seq, sl] = t[:, :FN_HEAD_DIM].astype(BF16)
            z_ref[seq:2 * seq, sl] = t[:, FN_HEAD_DIM:].astype(BF16)

    scale = 1.0 / math.sqrt(seq * FN_HEAD_DIM)
    re = jnp.dot(dfts_ref[...], z_ref[...], preferred_element_type=F32) * scale
    ya = jnp.dot(re.astype(BF16), wf_ref[...], preferred_element_type=F32)
    o_ref[0] = _rms(ya, g_ref[...]).astype(BF16)


def _dft_tables(seq):
    n = jnp.arange(seq, dtype=jnp.int32)
    ang_s = ((n[:, None] * n[None, :]) % seq).astype(F32) * (2.0 * math.pi / seq)
    dft_s = jnp.concatenate([jnp.cos(ang_s), -jnp.sin(ang_s)], axis=1).astype(BF16)
    c = jnp.arange(FN_HEAD_DIM, dtype=jnp.int32)
    ang_c = ((c[:, None] * c[None, :]) % FN_HEAD_DIM).astype(F32) * (2.0 * math.pi / FN_HEAD_DIM)
    dft_c = jnp.concatenate([jnp.cos(ang_c), jnp.sin(ang_c)], axis=1).astype(BF16)
    return dft_s, dft_c


def _fnet_mix(proj3, dft_s, dft_c, w_f, gain, *, tm=512):
    bn, seq, _ = proj3.shape
    return pl.pallas_call(
        functools.partial(_fnet_kernel, seq=seq),
        grid=(bn, seq // tm),
        in_specs=[
            pl.BlockSpec((1, seq, GROUP_WIDTH), lambda b, r: (b, 0, 0)),
            pl.BlockSpec((FN_HEAD_DIM, 2 * FN_HEAD_DIM), lambda b, r: (0, 0)),
            pl.BlockSpec((tm, 2 * seq), lambda b, r: (r, 0)),
            pl.BlockSpec((GROUP_WIDTH, GROUP_WIDTH), lambda b, r: (0, 0)),
            pl.BlockSpec((1, GROUP_WIDTH), lambda b, r: (0, 0)),
        ],
        out_specs=pl.BlockSpec((1, tm, GROUP_WIDTH), lambda b, r: (b, r, 0)),
        out_shape=jax.ShapeDtypeStruct((bn, seq, GROUP_WIDTH), BF16),
        scratch_shapes=[pltpu.VMEM((2 * seq, GROUP_WIDTH), BF16)],
        compiler_params=_cparams(("parallel", "arbitrary")),
        name="fnet_mix",
    )(proj3, dft_c, dft_s, w_f.astype(BF16), gain.reshape(1, GROUP_WIDTH))


S5_CHUNK = 256
S5_SEGS = 8
S5_STEPS = S5_CHUNK // S5_SEGS
S5_TILES_PER_PASS = 4


def _s5_params(lam_re, lam_im, log_step, b_re, b_im, c_re, c_im):
    step = jnp.exp(log_step.astype(F32))[..., None]
    lr = lam_re.astype(F32)
    li = lam_im.astype(F32)
    ar = lr * step
    ai = li * step
    mag = jnp.exp(ar)
    lbr = mag * jnp.cos(ai)
    lbi = mag * jnp.sin(ai)
    den = lr * lr + li * li
    qr = ((lbr - 1.0) * lr + lbi * li) / den
    qi = (lbi * lr - (lbr - 1.0) * li) / den
    bbr = qr[..., None] * b_re - qi[..., None] * b_im
    bbi = qr[..., None] * b_im + qi[..., None] * b_re
    eye = jnp.eye(S5_GROUPS, dtype=F32)

    def in_blk(b):
        t = jnp.einsum('dgph,gk->dghkp', b, eye)
        return t.reshape(2, GROUP_WIDTH, S5_NSTATE)

    def out_blk(c):
        t = jnp.einsum('dghp,gk->dgpkh', c, eye)
        return t.reshape(2, S5_NSTATE, GROUP_WIDTH)

    b_blk = jnp.concatenate([in_blk(bbr), in_blk(bbi)], axis=2).astype(BF16)
    c_blk = jnp.concatenate([out_blk(c_re.astype(F32)), -out_blk(c_im.astype(F32))], axis=1).astype(BF16)
    kpow = jnp.arange(1, S5_STEPS + 1, dtype=F32)[None, :, None, None]
    pmag = jnp.exp(ar[:, None] * kpow)
    pang = ai[:, None] * kpow
    pw = jnp.concatenate([(pmag * jnp.cos(pang)).reshape(2, S5_STEPS, S5_NSTATE),
                          (pmag * jnp.sin(pang)).reshape(2, S5_STEPS, S5_NSTATE)], axis=2)
    lam1 = pw[:, 0:1, :]
    pw_tiles = pw.reshape(2, S5_STEPS, 2 * S5_NSTATE // LANES, LANES).transpose(0, 2, 1, 3)
    return b_blk, c_blk, lam1, pw_tiles


def _s5_scan_kernel(uf_ref, ub_ref, bblk_ref, cblk_ref, lam_ref, pw_ref, yf_ref, yb_ref,
                    bu_ref, carry_ref, cin_ref):
    ns = S5_NSTATE

    @pl.when(pl.program_id(1) == 0)
    def _():
        carry_ref[...] = jnp.zeros_like(carry_ref)

    nt = ns // LANES
    for d, u_ref in ((0, uf_ref), (1, ub_ref)):
        bu = jnp.dot(u_ref[0].astype(BF16), bblk_ref[d], preferred_element_type=F32)
        for q in range(2 * nt):
            bu_ref[d, q] = bu[:, q * LANES:(q + 1) * LANES]

    def lanes(q):
        return slice(q * LANES, (q + 1) * LANES)

    for q0 in range(0, nt, S5_TILES_PER_PASS):
        tiles = range(q0, q0 + S5_TILES_PER_PASS)
        lam = {(d, q): (lam_ref[d, :, lanes(q)], lam_ref[d, :, lanes(nt + q)])
               for d in range(2) for q in tiles}

        def step(i, hs, tiles=tiles, lam=lam):
            out = []
            for d in range(2):
                row = i if d == 0 else S5_STEPS - 1 - i
                idx = pl.ds(row, S5_SEGS, stride=S5_STEPS)
                for q in tiles:
                    hr, hi = hs[len(out)], hs[len(out) + 1]
                    lr, li = lam[d, q]
                    nr = lr * hr - li * hi + bu_ref[d, q, idx, :]
                    ni = lr * hi + li * hr + bu_ref[d, nt + q, idx, :]
                    bu_ref[d, q, idx, :] = nr
                    bu_ref[d, nt + q, idx, :] = ni
                    out += [nr, ni]
            return tuple(out)

        z = jnp.zeros((S5_SEGS, LANES), F32)
        ends = lax.fori_loop(0, S5_STEPS, step, (z,) * (4 * S5_TILES_PER_PASS))

        cin = {}
        pos = 0
        for d in range(2):
            order = range(S5_SEGS) if d == 0 else range(S5_SEGS - 1, -1, -1)
            for q in tiles:
                er, ei = ends[pos], ends[pos + 1]
                pos += 2
                pr = pw_ref[d, q, S5_STEPS - 1:S5_STEPS, :]
                pi = pw_ref[d, nt + q, S5_STEPS - 1:S5_STEPS, :]
                cr = carry_ref[d, :, lanes(q)]
                ci = carry_ref[d, :, lanes(nt + q)]
                for sg in order:
                    cin_ref[d, sg:sg + 1, lanes(q)] = cr
                    cin_ref[d, sg:sg + 1, lanes(nt + q)] = ci
                    nr = er[sg:sg + 1] + pr * cr - pi * ci
                    ni = ei[sg:sg + 1] + pr * ci + pi * cr
                    cr, ci = nr, ni
                carry_ref[d, :, lanes(q)] = cr
                carry_ref[d, :, lanes(nt + q)] = ci
                cin[d, q] = (cin_ref[d, :, lanes(q)], cin_ref[d, :, lanes(nt + q)])

        def fix(i, c, tiles=tiles, cin=cin):
            for d in range(2):
                row = i if d == 0 else S5_STEPS - 1 - i
                idx = pl.ds(row, S5_SEGS, stride=S5_STEPS)
                for q in tiles:
                    pr = pw_ref[d, q, pl.ds(i, 1), :]
                    pi = pw_ref[d, nt + q, pl.ds(i, 1), :]
                    cr, ci = cin[d, q]
                    bu_ref[d, q, idx, :] = bu_ref[d, q, idx, :] + pr * cr - pi * ci
                    bu_ref[d, nt + q, idx, :] = bu_ref[d, nt + q, idx, :] + pr * ci + pi * cr
            return c

        lax.fori_loop(0, S5_STEPS, fix, 0)

    for d, y_ref in ((0, yf_ref), (1, yb_ref)):
        hs = jnp.concatenate([bu_ref[d, q].astype(BF16) for q in range(2 * nt)], axis=1)
        y_ref[0] = jnp.dot(hs, cblk_ref[d], preferred_element_type=F32)


def _s5_scan(proj3, b_blk, c_blk, lam1, pw):
    bn, seq, _ = proj3.shape
    nc = seq // S5_CHUNK
    ns2 = 2 * S5_NSTATE
    u_blk = (1, S5_CHUNK, GROUP_WIDTH)
    return pl.pallas_call(
        _s5_scan_kernel,
        grid=(bn, nc),
        in_specs=[
            pl.BlockSpec(u_blk, lambda b, c: (b, c, 1)),
            pl.BlockSpec(u_blk, lambda b, c: (b, nc - 1 - c, 1)),
            pl.BlockSpec((2, GROUP_WIDTH, ns2), lambda b, c: (0, 0, 0)),
            pl.BlockSpec((2, ns2, GROUP_WIDTH), lambda b, c: (0, 0, 0)),
            pl.BlockSpec((2, 1, ns2), lambda b, c: (0, 0, 0)),
            pl.BlockSpec((2, ns2 // LANES, S5_STEPS, LANES), lambda b, c: (0, 0, 0, 0)),
        ],
        out_specs=[
            pl.BlockSpec(u_blk, lambda b, c: (b, c, 0)),
            pl.BlockSpec(u_blk, lambda b, c: (b, nc - 1 - c, 0)),
        ],
        out_shape=[jax.ShapeDtypeStruct((bn, seq, GROUP_WIDTH), F32)] * 2,
        scratch_shapes=[
            pltpu.VMEM((2, ns2 // LANES, S5_CHUNK, LANES), F32),
            pltpu.VMEM((2, 1, ns2), F32),
            pltpu.VMEM((2, S5_SEGS, ns2), F32),
        ],
        compiler_params=_cparams(("parallel", "arbitrary")),
        name="s5_scan",
    )(proj3, proj3, b_blk, c_blk, lam1, pw)


def _s5_tail_kernel(yf_ref, yb_ref, u_ref, d_ref, wg_ref, g_ref, o_ref):
    y = yf_ref[...] + yb_ref[...] + d_ref[...] * u_ref[...]
    y = jax.nn.gelu(y)
    gate = jnp.dot(y.astype(BF16), wg_ref[...], preferred_element_type=F32)
    o_ref[...] = _rms(y * jax.nn.sigmoid(gate), g_ref[...]).astype(BF16)


def _s5_tail(yf, yb, proj, d_skip, w_glu, gain, *, tm=512):
    m = yf.shape[0]
    row = pl.BlockSpec((tm, GROUP_WIDTH), lambda i: (i, 0))
    vec = pl.BlockSpec((1, GROUP_WIDTH), lambda i: (0, 0))
    return pl.pallas_call(
        _s5_tail_kernel,
        grid=(m // tm,),
        in_specs=[row, row, pl.BlockSpec((tm, GROUP_WIDTH), lambda i: (i, 1)), vec,
                  pl.BlockSpec((GROUP_WIDTH, GROUP_WIDTH), lambda i: (0, 0)), vec],
        out_specs=row,
        out_shape=jax.ShapeDtypeStruct((m, GROUP_WIDTH), BF16),
        compiler_params=_cparams(("parallel",)),
        name="s5_tail",
    )(yf, yb, proj, d_skip.reshape(1, GROUP_WIDTH), w_glu.astype(BF16), gain.reshape(1, GROUP_WIDTH))


def _rope(x, cc, ss):
    return x * cc + pltpu.roll(x, HEAD_DIM // 2, axis=1) * ss


def _attn_kernel(q_ref, k_ref, v_ref, qg_ref, kg_ref, ccq_ref, ssq_ref, cck_ref, ssk_ref,
                 o_ref, kp_ref, vp_ref):
    @pl.when(pl.program_id(2) == 0)
    def _():
        kn = _rms(k_ref[0], kg_ref[...])
        kp_ref[...] = _rope(kn, cck_ref[...], ssk_ref[...]).astype(BF16)
        vp_ref[...] = v_ref[0].astype(BF16)

    rep = ATT_HEADS // KV_HEADS
    outs = []
    for r in range(rep):
        q = q_ref[0, :, r * HEAD_DIM:(r + 1) * HEAD_DIM]
        qn = _rope(_rms(q, qg_ref[...]), ccq_ref[...], ssq_ref[...]) * (HEAD_DIM ** -0.5)
        s = lax.dot_general(qn.astype(BF16), kp_ref[...], (((1,), (1,)), ((), ())),
                            preferred_element_type=F32)
        p = jnp.exp(s - jnp.max(s, axis=-1, keepdims=True))
        inv = 1.0 / jnp.sum(p, axis=-1, keepdims=True)
        o = jnp.dot(p.astype(BF16), vp_ref[...], preferred_element_type=F32) * inv
        outs.append(o)
    o_ref[0] = jnp.concatenate(outs, axis=1)


def _rope_tables(seq):
    rows = seq // GRID_W
    row_id = jnp.repeat(jnp.arange(rows), GRID_W).astype(F32)
    col_id = jnp.tile(jnp.arange(GRID_W), rows).astype(F32)
    inv = ROPE_THETA ** (-jnp.arange(0, AXIS_ROT_DIM, 2, dtype=F32) / AXIS_ROT_DIM)
    ang = jnp.concatenate([row_id[:, None] * inv, col_id[:, None] * inv], axis=-1)
    cos, sin = jnp.cos(ang), jnp.sin(ang)
    return jnp.concatenate([cos, cos], axis=1), jnp.concatenate([-sin, sin], axis=1)


def _attention_mix(proj3, q_gain, k_gain, cc, ss, *, tq=256):
    bn, seq, _ = proj3.shape
    rep = ATT_HEADS // KV_HEADS
    qw = rep * HEAD_DIM
    q_col0 = (2 * GROUP_WIDTH) // qw
    k_col0 = (3 * GROUP_WIDTH) // HEAD_DIM
    v_col0 = k_col0 + KV_HEADS
    vec = pl.BlockSpec((1, HEAD_DIM), lambda b, g, i: (0, 0))
    return pl.pallas_call(
        _attn_kernel,
        grid=(bn, KV_HEADS, seq // tq),
        in_specs=[
            pl.BlockSpec((1, tq, qw), lambda b, g, i: (b, i, q_col0 + g)),
            pl.BlockSpec((1, seq, HEAD_DIM), lambda b, g, i: (b, 0, k_col0 + g)),
            pl.BlockSpec((1, seq, HEAD_DIM), lambda b, g, i: (b, 0, v_col0 + g)),
            vec, vec,
            pl.BlockSpec((tq, HEAD_DIM), lambda b, g, i: (i, 0)),
            pl.BlockSpec((tq, HEAD_DIM), lambda b, g, i: (i, 0)),
            pl.BlockSpec((seq, HEAD_DIM), lambda b, g, i: (0, 0)),
            pl.BlockSpec((seq, HEAD_DIM), lambda b, g, i: (0, 0)),
        ],
        out_specs=pl.BlockSpec((1, tq, qw), lambda b, g, i: (b, i, g)),
        out_shape=jax.ShapeDtypeStruct((bn, seq, GROUP_WIDTH), F32),
        scratch_shapes=[pltpu.VMEM((seq, HEAD_DIM), BF16), pltpu.VMEM((seq, HEAD_DIM), BF16)],
        compiler_params=_cparams(("parallel", "parallel", "arbitrary")),
        name="gqa_attention",
    )(proj3, proj3, proj3, q_gain.reshape(1, HEAD_DIM), k_gain.reshape(1, HEAD_DIM),
      cc, ss, cc, ss)


def _head_ones(width):
    r = lax.broadcasted_iota(jnp.int32, (width, width), 0) // RW_HEAD_DIM
    c = lax.broadcasted_iota(jnp.int32, (width, width), 1) // RW_HEAD_DIM
    return (r == c).astype(BF16)


def _head_sum2(x, ones):
    hi, lo = _split2(x)
    return (jnp.dot(hi, ones, preferred_element_type=F32)
            + jnp.dot(lo, ones, preferred_element_type=F32))


def _rwkv_prep_kernel(c_ref, p_ref, n_ref, mu_ref, vecs_ref, dvec_ref,
                      g1_ref, g2_ref, w1_ref, w2_ref, a1_ref, a2_ref,
                      r_o, v_o, nkk_o, g_o, bonus_o, w_o, kd_o, b_o, *, tiles_per_seq):
    i = pl.program_id(0)
    tm = c_ref.shape[0]
    first = (i % tiles_per_seq) == 0
    last = (i % tiles_per_seq) == tiles_per_seq - 1
    rows = lax.broadcasted_iota(jnp.int32, (tm, GROUP_WIDTH), 0)
    ones = _head_ones(GROUP_WIDTH)

    def shifted(j):
        sl = slice(j * GROUP_WIDTH, (j + 1) * GROUP_WIDTH)
        x = c_ref[:, sl]
        prev_row = jnp.where(first, 0.0, p_ref[7:8, sl])
        next_row = jnp.where(last, 0.0, n_ref[0:1, sl])
        prev = jnp.where(rows == 0, prev_row, pltpu.roll(x, 1, axis=0))
        nxt = jnp.where(rows == tm - 1, next_row, pltpu.roll(x, tm - 1, axis=0))
        return x + (prev - x) * mu_ref[j, 0:1, :] + (nxt - x) * mu_ref[j, 1:2, :]

    r, k, v, z = (shifted(j) for j in range(4))
    k_k, k_a, r_k = vecs_ref[0:1, :], vecs_ref[1:2, :], vecs_ref[2:3, :]
    zb = z.astype(BF16)

    gmid = jax.nn.sigmoid(jnp.dot(zb, g1_ref[...], preferred_element_type=F32))
    g_o[...] = jnp.dot(gmid.astype(BF16), g2_ref[...], preferred_element_type=F32)

    kk = k * k_k
    kk = kk * lax.rsqrt(_head_sum2(kk * kk, ones) + 1e-12)
    r_o[...] = r
    v_o[...] = v
    nkk_o[...] = -kk

    bonus = jnp.zeros_like(r)
    for d in range(2):
        w0, a0 = dvec_ref[d, 0:1, :], dvec_ref[d, 1:2, :]
        wmid = jnp.tanh(jnp.dot(zb, w1_ref[d], preferred_element_type=F32))
        wlin = w0 + jnp.dot(wmid.astype(BF16), w2_ref[d], preferred_element_type=F32)
        w_log = -jax.nn.softplus(-wlin) - 0.5
        w_o[d] = jnp.exp(-jnp.exp(w_log))
        amid = jnp.dot(zb, a1_ref[d], preferred_element_type=F32)
        a = jax.nn.sigmoid(a0 + jnp.dot(amid.astype(BF16), a2_ref[d], preferred_element_type=F32))
        kd = k * (1.0 + (a - 1.0) * k_a)
        kd_o[d] = kd
        b_o[d] = kk * a
        bonus = bonus + _head_sum2(r * kd * r_k, ones) * v
    bonus_o[...] = bonus


def _pad_cols(w, n):
    return jnp.pad(w, [(0, 0)] * (w.ndim - 1) + [(0, n - w.shape[-1])])


def _pad_rows(w, n):
    return jnp.pad(w, [(0, 0)] * (w.ndim - 2) + [(0, n - w.shape[-2]), (0, 0)])


def _rwkv_prep(proj, seq, mu, w0, w1, w2, a0, a1, a2, g1, g2, k_k, k_a, r_k, *, tm=512):
    m = proj.shape[0]
    wide = 4 * GROUP_WIDTH
    tiles_per_seq = seq // tm
    nblk8 = m // 8
    vecs = jnp.stack([k_k, k_a, r_k.reshape(GROUP_WIDTH)]
                     + [jnp.zeros((GROUP_WIDTH,), F32)] * 5).astype(F32)
    dvec = jnp.stack([jnp.stack([w0[d], a0[d]] + [jnp.zeros((GROUP_WIDTH,), F32)] * 6)
                      for d in range(2)]).astype(F32)
    row = pl.BlockSpec((tm, GROUP_WIDTH), lambda i: (i, 0))
    row2 = pl.BlockSpec((2, tm, GROUP_WIDTH), lambda i: (0, i, 0))
    full = lambda a: pl.BlockSpec(a.shape, lambda i: (0,) * a.ndim)
    g1p = _pad_cols(g1, LORA_PAD).astype(BF16)
    g2p = _pad_rows(g2, LORA_PAD).astype(BF16)
    w1p = _pad_cols(w1, LORA_PAD).astype(BF16)
    w2p = _pad_rows(w2, LORA_PAD).astype(BF16)
    a1p = _pad_cols(a1, LORA_PAD).astype(BF16)
    a2p = _pad_rows(a2, LORA_PAD).astype(BF16)
    sds = jax.ShapeDtypeStruct
    return pl.pallas_call(
        functools.partial(_rwkv_prep_kernel, tiles_per_seq=tiles_per_seq),
        grid=(m // tm,),
        in_specs=[
            pl.BlockSpec((tm, wide), lambda i: (i, 1)),
            pl.BlockSpec((8, wide), lambda i: (jnp.maximum(i * (tm // 8) - 1, 0), 1)),
            pl.BlockSpec((8, wide), lambda i: (jnp.minimum((i + 1) * (tm // 8), nblk8 - 1), 1)),
            full(mu), full(vecs), full(dvec),
            full(g1p), full(g2p), full(w1p), full(w2p), full(a1p), full(a2p),
        ],
        out_specs=[row, row, row, row, row, row2, row2, row2],
        out_shape=[sds((m, GROUP_WIDTH), F32)] * 5 + [sds((2, m, GROUP_WIDTH), F32)] * 3,
        compiler_params=_cparams(("parallel",)),
        name="rwkv_prep",
    )(proj, proj, proj, mu, vecs, dvec, g1p, g2p, w1p, w2p, a1p, a2p)


RW_CHUNK = 64


def _rwkv_scan_kernel(rf, vf, nf, wf, kf, bf, rb, vb, nb, wb, kb, bb, yf_ref, yb_ref, st_ref,
                      *, bn):
    @pl.when(pl.program_id(0) == 0)
    def _():
        st_ref[...] = jnp.zeros_like(st_ref)

    ones = _head_ones(2 * LANES)
    vrow = lax.broadcasted_iota(jnp.int32, (RW_HEAD_DIM, GROUP_WIDTH), 0)
    lane = lax.broadcasted_iota(jnp.int32, (RW_HEAD_DIM, GROUP_WIDTH), 1)
    diag = (lane % RW_HEAD_DIM) == vrow

    def head_sum(hi, lo):
        parts = []
        for c in range(GROUP_WIDTH // (2 * LANES)):
            sl = slice(c * 2 * LANES, (c + 1) * 2 * LANES)
            acc = jnp.dot(hi[:, sl], ones, preferred_element_type=F32)
            if lo is not None:
                acc = acc + jnp.dot(lo[:, sl], ones, preferred_element_type=F32)
            parts.append(acc)
        return jnp.concatenate(parts, axis=1)

    chains = []
    for b in range(bn):
        chains.append((b, 0, (rf, vf, nf, wf, kf, bf), yf_ref))
        chains.append((b, 1, (rb, vb, nb, wb, kb, bb), yb_ref))

    def step(i, c):
        for b, d, refs, y_ref in chains:
            r_ref, v_ref, n_ref, w_ref, k_ref, b_ref = refs
            row = i if d == 0 else RW_CHUNK - 1 - i
            at = lambda ref: ref[b, pl.ds(row, 1), :]
            st = st_ref[2 * b + d]
            sa = head_sum(*_split2(st * at(n_ref)))
            vcol = head_sum(*_split2(jnp.where(diag, at(v_ref), 0.0)))
            st = st * at(w_ref) + sa * at(b_ref) + vcol * at(k_ref)
            st_ref[2 * b + d] = st
            yb_all = head_sum(*_split2(st * at(r_ref)))
            y_ref[b, pl.ds(row, 1), :] = jnp.sum(jnp.where(diag, yb_all, 0.0), axis=0, keepdims=True)
        return c

    lax.fori_loop(0, RW_CHUNK, step, 0)


def _rwkv_scan(r, v, nkk, w, kd, b):
    bn, seq, _ = r.shape
    nc = seq // RW_CHUNK
    blk = (bn, RW_CHUNK, GROUP_WIDTH)
    fwd = pl.BlockSpec(blk, lambda c: (0, c, 0))
    bwd = pl.BlockSpec(blk, lambda c: (0, nc - 1 - c, 0))
    fwd_d = pl.BlockSpec((None,) + blk, lambda c: (0, 0, c, 0))
    bwd_d = pl.BlockSpec((None,) + blk, lambda c: (1, 0, nc - 1 - c, 0))
    return pl.pallas_call(
        functools.partial(_rwkv_scan_kernel, bn=bn),
        grid=(nc,),
        in_specs=[fwd, fwd, fwd, fwd_d, fwd_d, fwd_d, bwd, bwd, bwd, bwd_d, bwd_d, bwd_d],
        out_specs=[fwd, bwd],
        out_shape=[jax.ShapeDtypeStruct((bn, seq, GROUP_WIDTH), F32)] * 2,
        scratch_shapes=[pltpu.VMEM((2 * bn, RW_HEAD_DIM, GROUP_WIDTH), F32)],
        compiler_params=_cparams(("arbitrary",)),
        name="rwkv_scan",
    )(r, v, nkk, w, kd, b, r, v, nkk, w, kd, b)


def _rwkv_post_kernel(yf_ref, yb_ref, bonus_ref, g_ref, lnw_ref, lnb_ref, gain_ref, o_ref):
    ones = _head_ones(GROUP_WIDTH)
    y = yf_ref[...] + yb_ref[...]
    mean = _head_sum2(y, ones) * (1.0 / RW_HEAD_DIM)
    yc = y - mean
    var = _head_sum2(yc * yc, ones) * (1.0 / RW_HEAD_DIM)
    yn = yc * lax.rsqrt(var + RW_GN_EPS) * lnw_ref[...] + lnb_ref[...]
    out = (yn + bonus_ref[...]) * g_ref[...]
    o_ref[...] = _rms(out, gain_ref[...]).astype(BF16)


def _rwkv_post(yf, yb, bonus, g, ln_w, ln_b, gain, *, tm=512):
    m = yf.shape[0]
    row = pl.BlockSpec((tm, GROUP_WIDTH), lambda i: (i, 0))
    vec = pl.BlockSpec((1, GROUP_WIDTH), lambda i: (0, 0))
    v2 = lambda a: a.reshape(1, GROUP_WIDTH)
    return pl.pallas_call(
        _rwkv_post_kernel,
        grid=(m // tm,),
        in_specs=[row, row, row, row, vec, vec, vec],
        out_specs=row,
        out_shape=jax.ShapeDtypeStruct((m, GROUP_WIDTH), BF16),
        compiler_params=_cparams(("parallel",)),
        name="rwkv_post",
    )(yf, yb, bonus, g, v2(ln_w), v2(ln_b), v2(gain))


def _norm_rows_kernel(x_ref, g_ref, o_ref):
    o_ref[...] = _rms(x_ref[...], g_ref[...]).astype(BF16)


def _norm_rows(x, gain, *, tm=512):
    m, n = x.shape
    return pl.pallas_call(
        _norm_rows_kernel,
        grid=(m // tm,),
        in_specs=[pl.BlockSpec((tm, n), lambda i: (i, 0)), pl.BlockSpec((1, n), lambda i: (0, 0))],
        out_specs=pl.BlockSpec((tm, n), lambda i: (i, 0)),
        out_shape=jax.ShapeDtypeStruct((m, n), BF16),
        compiler_params=_cparams(("parallel",)),
        name="norm_rows",
    )(x, gain.reshape(1, n))


def _out_proj_kernel(ya_ref, yb_ref, yc_ref, yd_ref, w_ref, x_ref, o_ref):
    acc = x_ref[...]
    for g, y_ref in enumerate((ya_ref, yb_ref, yc_ref, yd_ref)):
        acc = acc + jnp.dot(y_ref[...], w_ref[g * GROUP_WIDTH:(g + 1) * GROUP_WIDTH, :],
                            preferred_element_type=F32)
    o_ref[...] = acc


def _out_proj(ya, yb, yc, yd, w_out, x, *, tm=512, tn=1024):
    m, n = x.shape
    row = pl.BlockSpec((tm, GROUP_WIDTH), lambda i, j: (i, 0))
    return pl.pallas_call(
        _out_proj_kernel,
        grid=(m // tm, n // tn),
        in_specs=[row, row, row, row,
                  pl.BlockSpec((N_MIXERS * GROUP_WIDTH, tn), lambda i, j: (0, j)),
                  pl.BlockSpec((tm, tn), lambda i, j: (i, j))],
        out_specs=pl.BlockSpec((tm, tn), lambda i, j: (i, j)),
        out_shape=jax.ShapeDtypeStruct((m, n), F32),
        compiler_params=_cparams(("parallel", "parallel")),
        name="out_proj",
    )(ya, yb, yc, yd, w_out.astype(BF16), x)


PEER_TOK = 256
NEG_INF = float("-inf")


def _peer_stats_kernel(q_ref, khi_ref, klo_ref, sc_o, ex_o, thr_o, top_ref, cand_ref):
    nt = q_ref.shape[0]
    for h in range(PEER_HEADS):
        qh = q_ref[:, h * LANES:(h + 1) * LANES]
        qhi, qlo = _split2(qh)
        nt_dot = lambda a, b: lax.dot_general(a, b, (((1,), (1,)), ((), ())),
                                              preferred_element_type=F32)
        for p in range(2):
            sc = (nt_dot(khi_ref[h, p], qhi) + nt_dot(khi_ref[h, p], qlo)
                  + nt_dot(klo_ref[h, p], qhi))
            sc_o[h, p] = sc
            x = sc
            for i in range(PEER_TOPK):
                m = jnp.max(x, axis=0, keepdims=True)
                top_ref[p, i:i + 1, :] = m
                x = jnp.where(x == m, NEG_INF, x)
        s2 = top_ref[1]
        for i in range(PEER_TOPK):
            cand_ref[i * PEER_TOPK:(i + 1) * PEER_TOPK, :] = top_ref[0, i:i + 1, :] + s2
        x = cand_ref[...]
        top = top_ref[0, 0:1, :] + top_ref[1, 0:1, :]
        zsum = jnp.zeros((1, nt), F32)
        m = top
        for i in range(PEER_TOPK):
            m = jnp.max(x, axis=0, keepdims=True)
            zsum = zsum + jnp.exp(m - top)
            x = jnp.where(x == m, NEG_INF, x)
        thr_o[h:h + 1, :] = m
        ex_o[h, 0] = jnp.exp(sc_o[h, 0] - top_ref[0, 0:1, :]) / zsum
        ex_o[h, 1] = jnp.exp(sc_o[h, 1] - top_ref[1, 0:1, :])


def _peer_stats(q, sub_keys):
    ntok = q.shape[0]
    keys = jnp.zeros((PEER_HEADS, 2, PEER_N_KEYS, LANES), F32)
    keys = keys.at[:, 0, :, :PEER_HALF].set(sub_keys[:, 0]).at[:, 1, :, PEER_HALF:].set(sub_keys[:, 1])
    khi = keys.astype(BF16)
    klo = (keys - khi.astype(F32)).astype(BF16)
    kspec = pl.BlockSpec(keys.shape, lambda i: (0, 0, 0, 0))
    big = pl.BlockSpec((PEER_HEADS, 2, PEER_N_KEYS, PEER_TOK), lambda i: (0, 0, 0, i))
    sds = jax.ShapeDtypeStruct
    return pl.pallas_call(
        _peer_stats_kernel,
        grid=(ntok // PEER_TOK,),
        in_specs=[pl.BlockSpec((PEER_TOK, PEER_HEADS * LANES), lambda i: (i, 0)), kspec, kspec],
        out_specs=[big, big, pl.BlockSpec((PEER_HEADS, PEER_TOK), lambda i: (0, i))],
        out_shape=[sds((PEER_HEADS, 2, PEER_N_KEYS, ntok), F32)] * 2 + [sds((PEER_HEADS, ntok), F32)],
        scratch_shapes=[pltpu.VMEM((2, PEER_TOPK, PEER_TOK), F32),
                        pltpu.VMEM((PEER_TOPK * PEER_TOPK, PEER_TOK), F32)],
        compiler_params=_cparams(("parallel",)),
        name="peer_stats",
    )(q, khi, klo)


PEER_TB = 512
PEER_ET = 256


def _peer_expert_kernel(hn_ref, u_ref, v_ref, sc_ref, ex_ref, thr_ref, x_ref, o_ref, w_ref):
    j = pl.program_id(1)

    @pl.when(j == 0)
    def _():
        o_ref[...] = x_ref[...]

    act = lax.dot_general(u_ref[...], hn_ref[...], (((1,), (1,)), ((), ())),
                          preferred_element_type=F32)
    gel = 0.5 * act * (1.0 + lax.erf(act * (1.0 / math.sqrt(2.0))))
    for ai in range(PEER_ET // PEER_N_KEYS):
        a = j * (PEER_ET // PEER_N_KEYS) + ai
        sc1 = [sc_ref[h, 0, pl.ds(a, 1), :] for h in range(PEER_HEADS)]
        ex1 = [ex_ref[h, 0, pl.ds(a, 1), :] for h in range(PEER_HEADS)]
        for t in range(PEER_TB // LANES):
            tl = slice(t * LANES, (t + 1) * LANES)
            gate = jnp.zeros((PEER_N_KEYS, LANES), F32)
            for h in range(PEER_HEADS):
                s = sc1[h][:, tl] + sc_ref[h, 1, :, tl]
                keep = s >= thr_ref[h:h + 1, tl]
                gate = gate + jnp.where(keep, ex_ref[h, 1, :, tl], 0.0) * ex1[h][:, tl]
            rows = slice(ai * PEER_N_KEYS, (ai + 1) * PEER_N_KEYS)
            w_ref[rows, tl] = (gate * gel[rows, tl]).astype(BF16)
    o_ref[...] += lax.dot_general(w_ref[...], v_ref[...], (((0,), (0,)), ((), ())),
                                  preferred_element_type=F32)


def _peer_experts(hn, u, v, sc, ex, thr, x):
    ntok, d = x.shape
    nexp = u.shape[0]
    stat = pl.BlockSpec((PEER_HEADS, 2, PEER_N_KEYS, PEER_TB), lambda i, j: (0, 0, 0, i))
    tok = lambda dt: pl.BlockSpec((PEER_TB, d), lambda i, j: (i, 0))
    return pl.pallas_call(
        _peer_expert_kernel,
        grid=(ntok // PEER_TB, nexp // PEER_ET),
        in_specs=[
            tok(BF16),
            pl.BlockSpec((PEER_ET, d), lambda i, j: (j, 0)),
            pl.BlockSpec((PEER_ET, d), lambda i, j: (j, 0)),
            stat, stat,
            pl.BlockSpec((PEER_HEADS, PEER_TB), lambda i, j: (0, i)),
            tok(F32),
        ],
        out_specs=tok(F32),
        out_shape=jax.ShapeDtypeStruct((ntok, d), F32),
        scratch_shapes=[pltpu.VMEM((PEER_ET, PEER_TB), BF16)],
        compiler_params=_cparams(("parallel", "arbitrary")),
        name="peer_experts",
    )(hn, u, v, sc, ex, thr, x)


def _peer_ffn(x, norm_gain, w_query, sub_keys, expert_u, expert_v):
    q, hn = _norm_matmul(x, norm_gain, w_query.astype(BF16))
    sc, ex, thr = _peer_stats(q, sub_keys)
    return _peer_experts(hn, expert_u.astype(BF16), expert_v.astype(BF16), sc, ex, thr, x)


def kernel(x, norm_mix, w_in, fourier_w, s5_lam_re, s5_lam_im, s5_log_step, s5_b_re, s5_b_im, s5_c_re, s5_c_im, s5_d, s5_w_glu, attn_q_norm, attn_k_norm, rwkv_mu, rwkv_w0, rwkv_w1, rwkv_w2, rwkv_a0, rwkv_a1, rwkv_a2, rwkv_g1, rwkv_g2, rwkv_k_k, rwkv_k_a, rwkv_r_k, rwkv_ln_w, rwkv_ln_b, branch_norm, w_out, norm_ffn, peer_w_query, peer_sub_keys, peer_u, peer_v):
    bn, seq, d = x.shape
    m = bn * seq
    depth = w_in.shape[0]
    dft_s, dft_c = _dft_tables(seq)
    cc, ss = _rope_tables(seq)
    xf = x.reshape(m, d)
    for l in range(depth):
        bgain = branch_norm[l].reshape(N_MIXERS, GROUP_WIDTH)
        proj, _ = _norm_matmul(xf, norm_mix[l], w_in[l].astype(BF16))
        proj3 = proj.reshape(bn, seq, PROJ_WIDTH)

        ya = _fnet_mix(proj3, dft_s, dft_c, fourier_w[l], bgain[0]).reshape(m, GROUP_WIDTH)

        b_blk, c_blk, lam1, pw = _s5_params(s5_lam_re[l], s5_lam_im[l], s5_log_step[l],
                                            s5_b_re[l], s5_b_im[l], s5_c_re[l], s5_c_im[l])
        yf, yb = _s5_scan(proj3, b_blk, c_blk, lam1, pw)
        ybm = _s5_tail(yf.reshape(m, GROUP_WIDTH), yb.reshape(m, GROUP_WIDTH), proj,
                       s5_d[l], s5_w_glu[l], bgain[1])

        att = _attention_mix(proj3, attn_q_norm[l], attn_k_norm[l], cc, ss)
        yc = _norm_rows(att.reshape(m, GROUP_WIDTH), bgain[2])

        r, v, nkk, g, bonus, w, kd, b = _rwkv_prep(
            proj, seq, rwkv_mu[l], rwkv_w0[l], rwkv_w1[l], rwkv_w2[l], rwkv_a0[l], rwkv_a1[l],
            rwkv_a2[l], rwkv_g1[l], rwkv_g2[l], rwkv_k_k[l], rwkv_k_a[l], rwkv_r_k[l])
        s3 = lambda a: a.reshape(bn, seq, GROUP_WIDTH)
        s4 = lambda a: a.reshape(2, bn, seq, GROUP_WIDTH)
        y_f, y_b = _rwkv_scan(s3(r), s3(v), s3(nkk), s4(w), s4(kd), s4(b))
        yd = _rwkv_post(y_f.reshape(m, GROUP_WIDTH), y_b.reshape(m, GROUP_WIDTH), bonus, g,
                        rwkv_ln_w[l], rwkv_ln_b[l], bgain[3])

        xf = _out_proj(ya, ybm, yc, yd, w_out[l], xf)
        xf = _peer_ffn(xf, norm_ffn[l], peer_w_query[l], peer_sub_keys[l], peer_u[l], peer_v[l])
    return xf.reshape(bn, seq, d)
```

```python
import functools
import math

import jax
import jax.numpy as jnp
from jax import lax
from jax.experimental import pallas as pl
from jax.experimental.pallas import tpu as pltpu

F32 = jnp.float32
BF16 = jnp.bfloat16

D_MODEL = 2048
GROUP_WIDTH = 512
N_MIXERS = 4
FN_HEADS = 4
FN_HEAD_DIM = 128
S5_GROUP_CH = 16
S5_GROUPS = 32
S5_STATE = 64
S5_NSTATE = S5_GROUPS * S5_STATE
HEAD_DIM = 128
ATT_HEADS = 4
KV_HEADS = 2
GRID_W = 64
ROPE_THETA = 10000.0
AXIS_ROT_DIM = 64
RW_HEAD_DIM = 64
RW_HEADS = 8
RW_GN_EPS = 64e-5
LORA_PAD = 128
PEER_HEADS = 8
PEER_HALF = 64
PEER_N_KEYS = 128
PEER_TOPK = 16
NORM_EPS = 1e-6
PROJ_WIDTH = 4096

LANES = 128
VMEM_LIMIT = 56 * 1024 * 1024


def _cparams(sem):
    return pltpu.CompilerParams(dimension_semantics=sem, vmem_limit_bytes=VMEM_LIMIT)


def _rms(x, gain):
    return x * lax.rsqrt(jnp.mean(x * x, axis=-1, keepdims=True) + NORM_EPS) * gain


def _split2(x):
    hi = x.astype(BF16)
    lo = (x - hi.astype(F32)).astype(BF16)
    return hi, lo


def _norm_matmul_kernel(x_ref, g_ref, w_ref, o_ref, xn_out_ref, xn_ref):
    @pl.when(pl.program_id(1) == 0)
    def _():
        xn = _rms(x_ref[...], g_ref[...]).astype(BF16)
        xn_ref[...] = xn
        xn_out_ref[...] = xn

    o_ref[...] = jnp.dot(xn_ref[...], w_ref[...], preferred_element_type=F32)


def _norm_matmul(x, gain, w, *, tm=512, tn=1024):
    m, k = x.shape
    n = w.shape[1]
    return pl.pallas_call(
        _norm_matmul_kernel,
        grid=(m // tm, n // tn),
        in_specs=[
            pl.BlockSpec((tm, k), lambda i, j: (i, 0)),
            pl.BlockSpec((1, k), lambda i, j: (0, 0)),
            pl.BlockSpec((k, tn), lambda i, j: (0, j)),
        ],
        out_specs=[
            pl.BlockSpec((tm, tn), lambda i, j: (i, j)),
            pl.BlockSpec((tm, k), lambda i, j: (i, 0)),
        ],
        out_shape=[
            jax.ShapeDtypeStruct((m, n), F32),
            jax.ShapeDtypeStruct((m, k), BF16),
        ],
        scratch_shapes=[pltpu.VMEM((tm, k), BF16)],
        compiler_params=_cparams(("parallel", "arbitrary")),
        name="norm_matmul",
    )(x, gain.reshape(1, k), w)


def _fnet_kernel(pa_ref, dftc_ref, dfts_ref, wf_ref, g_ref, o_ref, z_ref, *, seq):
    @pl.when(pl.program_id(1) == 0)
    def _():
        for h in range(FN_HEADS):
            sl = slice(h * FN_HEAD_DIM, (h + 1) * FN_HEAD_DIM)
            xh = pa_ref[0, :, sl].astype(BF16)
            t = jnp.dot(xh, dftc_ref[...], preferred_element_type=F32)
            z_ref[0:seq, sl] = t[:, :FN_HEAD_DIM].astype(BF16)
            z_ref[seq:2 * seq, sl] = t[:, FN_HEAD_DIM:].astype(BF16)

    scale = 1.0 / math.sqrt(seq * FN_HEAD_DIM)
    re = jnp.dot(dfts_ref[...], z_ref[...], preferred_element_type=F32) * scale
    ya = jnp.dot(re.astype(BF16), wf_ref[...], preferred_element_type=F32)
    o_ref[0] = _rms(ya, g_ref[...]).astype(BF16)


def _dft_tables(seq):
    n = jnp.arange(seq, dtype=jnp.int32)
    ang_s = ((n[:, None] * n[None, :]) % seq).astype(F32) * (2.0 * math.pi / seq)
    dft_s = jnp.concatenate([jnp.cos(ang_s), -jnp.sin(ang_s)], axis=1).astype(BF16)
    c = jnp.arange(FN_HEAD_DIM, dtype=jnp.int32)
    ang_c = ((c[:, None] * c[None, :]) % FN_HEAD_DIM).astype(F32) * (2.0 * math.pi / FN_HEAD_DIM)
    dft_c = jnp.concatenate([jnp.cos(ang_c), jnp.sin(ang_c)], axis=1).astype(BF16)
    return dft_s, dft_c


def _fnet_mix(proj3, dft_s, dft_c, w_f, gain, *, tm=512):
    bn, seq, _ = proj3.shape
    return pl.pallas_call(
        functools.partial(_fnet_kernel, seq=seq),
        grid=(bn, seq // tm),
        in_specs=[
            pl.BlockSpec((1, seq, GROUP_WIDTH), lambda b, r: (b, 0, 0)),
            pl.BlockSpec((FN_HEAD_DIM, 2 * FN_HEAD_DIM), lambda b, r: (0, 0)),
            pl.BlockSpec((tm, 2 * seq), lambda b, r: (r, 0)),
            pl.BlockSpec((GROUP_WIDTH, GROUP_WIDTH), lambda b, r: (0, 0)),
            pl.BlockSpec((1, GROUP_WIDTH), lambda b, r: (0, 0)),
        ],
        out_specs=pl.BlockSpec((1, tm, GROUP_WIDTH), lambda b, r: (b, r, 0)),
        out_shape=jax.ShapeDtypeStruct((bn, seq, GROUP_WIDTH), BF16),
        scratch_shapes=[pltpu.VMEM((2 * seq, GROUP_WIDTH), BF16)],
        compiler_params=_cparams(("parallel", "arbitrary")),
        name="fnet_mix",
    )(proj3, dft_c, dft_s, w_f.astype(BF16), gain.reshape(1, GROUP_WIDTH))


S5_CHUNK = 256
S5_SEGS = 8
S5_STEPS = S5_CHUNK // S5_SEGS
S5_TILES_PER_PASS = 4


def _s5_params(lam_re, lam_im, log_step, b_re, b_im, c_re, c_im):
    step = jnp.exp(log_step.astype(F32))[..., None]
    lr = lam_re.astype(F32)
    li = lam_im.astype(F32)
    ar = lr * step
    ai = li * step
    mag = jnp.exp(ar)
    lbr = mag * jnp.cos(ai)
    lbi = mag * jnp.sin(ai)
    den = lr * lr + li * li
    qr = ((lbr - 1.0) * lr + lbi * li) / den
    qi = (lbi * lr - (lbr - 1.0) * li) / den
    bbr = qr[..., None] * b_re - qi[..., None] * b_im
    bbi = qr[..., None] * b_im + qi[..., None] * b_re
    eye = jnp.eye(S5_GROUPS, dtype=F32)

    def in_blk(b):
        t = jnp.einsum('dgph,gk->dghkp', b, eye)
        return t.reshape(2, GROUP_WIDTH, S5_NSTATE)

    def out_blk(c):
        t = jnp.einsum('dghp,gk->dgpkh', c, eye)
        return t.reshape(2, S5_NSTATE, GROUP_WIDTH)

    b_blk = jnp.concatenate([in_blk(bbr), in_blk(bbi)], axis=2).astype(BF16)
    c_blk = jnp.concatenate([out_blk(c_re.astype(F32)), -out_blk(c_im.astype(F32))], axis=1).astype(BF16)
    kpow = jnp.arange(1, S5_STEPS + 1, dtype=F32)[None, :, None, None]
    pmag = jnp.exp(ar[:, None] * kpow)
    pang = ai[:, None] * kpow
    pw = jnp.concatenate([(pmag * jnp.cos(pang)).reshape(2, S5_STEPS, S5_NSTATE),
                          (pmag * jnp.sin(pang)).reshape(2, S5_STEPS, S5_NSTATE)], axis=2)
    lam1 = pw[:, 0:1, :]
    pw_tiles = pw.reshape(2, S5_STEPS, 2 * S5_NSTATE // LANES, LANES).transpose(0, 2, 1, 3)
    return b_blk, c_blk, lam1, pw_tiles


def _s5_scan_kernel(uf_ref, ub_ref, bblk_ref, cblk_ref, lam_ref, pw_ref, yf_ref, yb_ref,
                    bu_ref, carry_ref, cin_ref):
    ns = S5_NSTATE

    @pl.when(pl.program_id(1) == 0)
    def _():
        carry_ref[...] = jnp.zeros_like(carry_ref)

    nt = ns // LANES
    for d, u_ref in ((0, uf_ref), (1, ub_ref)):
        bu = jnp.dot(u_ref[0].astype(BF16), bblk_ref[d], preferred_element_type=F32)
        for q in range(2 * nt):
            bu_ref[d, q] = bu[:, q * LANES:(q + 1) * LANES]

    def lanes(q):
        return slice(q * LANES, (q + 1) * LANES)

    for q0 in range(0, nt, S5_TILES_PER_PASS):
        tiles = range(q0, q0 + S5_TILES_PER_PASS)
        lam = {(d, q): (lam_ref[d, :, lanes(q)], lam_ref[d, :, lanes(nt + q)])
               for d in range(2) for q in tiles}

        def step(i, hs, tiles=tiles, lam=lam):
            out = []
            for d in range(2):
                row = i if d == 0 else S5_STEPS - 1 - i
                idx = pl.ds(pl.multiple_of(row * S5_SEGS, S5_SEGS), S5_SEGS)
                for q in tiles:
                    hr, hi = hs[len(out)], hs[len(out) + 1]
                    lr, li = lam[d, q]
                    nr = lr * hr - li * hi + bu_ref[d, q, idx, :]
                    ni = lr * hi + li * hr + bu_ref[d, nt + q, idx, :]
                    bu_ref[d, q, idx, :] = nr
                    bu_ref[d, nt + q, idx, :] = ni
                    out += [nr, ni]
            return tuple(out)

        z = jnp.zeros((S5_SEGS, LANES), F32)
        ends = lax.fori_loop(0, S5_STEPS, step, (z,) * (4 * S5_TILES_PER_PASS))

        cin = {}
        pos = 0
        for d in range(2):
            order = range(S5_SEGS) if d == 0 else range(S5_SEGS - 1, -1, -1)
            for q in tiles:
                er, ei = ends[pos], ends[pos + 1]
                pos += 2
                pr = pw_ref[d, q, S5_STEPS - 1:S5_STEPS, :]
                pi = pw_ref[d, nt + q, S5_STEPS - 1:S5_STEPS, :]
                cr = carry_ref[d, :, lanes(q)]
                ci = carry_ref[d, :, lanes(nt + q)]
                for sg in order:
                    cin_ref[d, sg:sg + 1, lanes(q)] = cr
                    cin_ref[d, sg:sg + 1, lanes(nt + q)] = ci
                    nr = er[sg:sg + 1] + pr * cr - pi * ci
                    ni = ei[sg:sg + 1] + pr * ci + pi * cr
                    cr, ci = nr, ni
                carry_ref[d, :, lanes(q)] = cr
                carry_ref[d, :, lanes(nt + q)] = ci
                cin[d, q] = (cin_ref[d, :, lanes(q)], cin_ref[d, :, lanes(nt + q)])

        def fix(i, c, tiles=tiles, cin=cin):
            for d in range(2):
                row = i if d == 0 else S5_STEPS - 1 - i
                idx = pl.ds(pl.multiple_of(row * S5_SEGS, S5_SEGS), S5_SEGS)
                for q in tiles:
                    pr = pw_ref[d, q, pl.ds(i, 1), :]
                    pi = pw_ref[d, nt + q, pl.ds(i, 1), :]
                    cr, ci = cin[d, q]
                    bu_ref[d, q, idx, :] = bu_ref[d, q, idx, :] + pr * cr - pi * ci
                    bu_ref[d, nt + q, idx, :] = bu_ref[d, nt + q, idx, :] + pr * ci + pi * cr
            return c

        lax.fori_loop(0, S5_STEPS, fix, 0)

    for d, y_ref in ((0, yf_ref), (1, yb_ref)):
        hs = jnp.concatenate([bu_ref[d, q].astype(BF16) for q in range(2 * nt)], axis=1)
        y_ref[0] = jnp.dot(hs, cblk_ref[d], preferred_element_type=F32)


def _s5_row_order(x, inverse=False):
    bn, seq, n = x.shape
    a, b = (S5_STEPS, S5_SEGS) if inverse else (S5_SEGS, S5_STEPS)
    return x.reshape(bn, seq // S5_CHUNK, a, b, n).transpose(0, 1, 3, 2, 4).reshape(bn, seq, n)


def _s5_scan(u, b_blk, c_blk, lam1, pw):
    bn, seq, _ = u.shape
    nc = seq // S5_CHUNK
    ns2 = 2 * S5_NSTATE
    u_blk = (1, S5_CHUNK, GROUP_WIDTH)
    return pl.pallas_call(
        _s5_scan_kernel,
        grid=(bn, nc),
        in_specs=[
            pl.BlockSpec(u_blk, lambda b, c: (b, c, 0)),
            pl.BlockSpec(u_blk, lambda b, c: (b, nc - 1 - c, 0)),
            pl.BlockSpec((2, GROUP_WIDTH, ns2), lambda b, c: (0, 0, 0)),
            pl.BlockSpec((2, ns2, GROUP_WIDTH), lambda b, c: (0, 0, 0)),
            pl.BlockSpec((2, 1, ns2), lambda b, c: (0, 0, 0)),
            pl.BlockSpec((2, ns2 // LANES, S5_STEPS, LANES), lambda b, c: (0, 0, 0, 0)),
        ],
        out_specs=[
            pl.BlockSpec(u_blk, lambda b, c: (b, c, 0)),
            pl.BlockSpec(u_blk, lambda b, c: (b, nc - 1 - c, 0)),
        ],
        out_shape=[jax.ShapeDtypeStruct((bn, seq, GROUP_WIDTH), F32)] * 2,
        scratch_shapes=[
            pltpu.VMEM((2, ns2 // LANES, S5_CHUNK, LANES), F32),
            pltpu.VMEM((2, 1, ns2), F32),
            pltpu.VMEM((2, S5_SEGS, ns2), F32),
        ],
        compiler_params=_cparams(("parallel", "arbitrary")),
        name="s5_scan",
    )(u, u, b_blk, c_blk, lam1, pw)


def _s5_tail_kernel(yf_ref, yb_ref, u_ref, d_ref, wg_ref, g_ref, o_ref):
    y = yf_ref[...] + yb_ref[...] + d_ref[...] * u_ref[...]
    y = jax.nn.gelu(y)
    gate = jnp.dot(y.astype(BF16), wg_ref[...], preferred_element_type=F32)
    o_ref[...] = _rms(y * jax.nn.sigmoid(gate), g_ref[...]).astype(BF16)


def _s5_tail(yf, yb, u, d_skip, w_glu, gain, *, tm=512):
    m = yf.shape[0]
    row = pl.BlockSpec((tm, GROUP_WIDTH), lambda i: (i, 0))
    vec = pl.BlockSpec((1, GROUP_WIDTH), lambda i: (0, 0))
    return pl.pallas_call(
        _s5_tail_kernel,
        grid=(m // tm,),
        in_specs=[row, row, row, vec,
                  pl.BlockSpec((GROUP_WIDTH, GROUP_WIDTH), lambda i: (0, 0)), vec],
        out_specs=row,
        out_shape=jax.ShapeDtypeStruct((m, GROUP_WIDTH), BF16),
        compiler_params=_cparams(("parallel",)),
        name="s5_tail",
    )(yf, yb, u, d_skip.reshape(1, GROUP_WIDTH), w_glu.astype(BF16), gain.reshape(1, GROUP_WIDTH))


def _rope(x, cc, ss):
    return x * cc + pltpu.roll(x, HEAD_DIM // 2, axis=1) * ss


def _attn_kernel(q_ref, k_ref, v_ref, qg_ref, kg_ref, ccq_ref, ssq_ref, cck_ref, ssk_ref,
                 o_ref, kp_ref, vp_ref):
    @pl.when(pl.program_id(2) == 0)
    def _():
        kn = _rms(k_ref[0], kg_ref[...])
        kp_ref[...] = _rope(kn, cck_ref[...], ssk_ref[...]).astype(BF16)
        vp_ref[...] = v_ref[0].astype(BF16)

    rep = ATT_HEADS // KV_HEADS
    outs = []
    for r in range(rep):
        q = q_ref[0, :, r * HEAD_DIM:(r + 1) * HEAD_DIM]
        qn = _rope(_rms(q, qg_ref[...]), ccq_ref[...], ssq_ref[...]) * (HEAD_DIM ** -0.5)
        s = lax.dot_general(qn.astype(BF16), kp_ref[...], (((1,), (1,)), ((), ())),
                            preferred_element_type=F32)
        p = jnp.exp(s - jnp.max(s, axis=-1, keepdims=True))
        inv = 1.0 / jnp.sum(p, axis=-1, keepdims=True)
        o = jnp.dot(p.astype(BF16), vp_ref[...], preferred_element_type=F32) * inv
        outs.append(o)
    o_ref[0] = jnp.concatenate(outs, axis=1)


def _rope_tables(seq):
    rows = seq // GRID_W
    row_id = jnp.repeat(jnp.arange(rows), GRID_W).astype(F32)
    col_id = jnp.tile(jnp.arange(GRID_W), rows).astype(F32)
    inv = ROPE_THETA ** (-jnp.arange(0, AXIS_ROT_DIM, 2, dtype=F32) / AXIS_ROT_DIM)
    ang = jnp.concatenate([row_id[:, None] * inv, col_id[:, None] * inv], axis=-1)
    cos, sin = jnp.cos(ang), jnp.sin(ang)
    return jnp.concatenate([cos, cos], axis=1), jnp.concatenate([-sin, sin], axis=1)


def _attention_mix(proj3, q_gain, k_gain, cc, ss, *, tq=256):
    bn, seq, _ = proj3.shape
    rep = ATT_HEADS // KV_HEADS
    qw = rep * HEAD_DIM
    q_col0 = (2 * GROUP_WIDTH) // qw
    k_col0 = (3 * GROUP_WIDTH) // HEAD_DIM
    v_col0 = k_col0 + KV_HEADS
    vec = pl.BlockSpec((1, HEAD_DIM), lambda b, g, i: (0, 0))
    return pl.pallas_call(
        _attn_kernel,
        grid=(bn, KV_HEADS, seq // tq),
        in_specs=[
            pl.BlockSpec((1, tq, qw), lambda b, g, i: (b, i, q_col0 + g)),
            pl.BlockSpec((1, seq, HEAD_DIM), lambda b, g, i: (b, 0, k_col0 + g)),
            pl.BlockSpec((1, seq, HEAD_DIM), lambda b, g, i: (b, 0, v_col0 + g)),
            vec, vec,
            pl.BlockSpec((tq, HEAD_DIM), lambda b, g, i: (i, 0)),
            pl.BlockSpec((tq, HEAD_DIM), lambda b, g, i: (i, 0)),
            pl.BlockSpec((seq, HEAD_DIM), lambda b, g, i: (0, 0)),
            pl.BlockSpec((seq, HEAD_DIM), lambda b, g, i: (0, 0)),
        ],
        out_specs=pl.BlockSpec((1, tq, qw), lambda b, g, i: (b, i, g)),
        out_shape=jax.ShapeDtypeStruct((bn, seq, GROUP_WIDTH), F32),
        scratch_shapes=[pltpu.VMEM((seq, HEAD_DIM), BF16), pltpu.VMEM((seq, HEAD_DIM), BF16)],
        compiler_params=_cparams(("parallel", "parallel", "arbitrary")),
        name="gqa_attention",
    )(proj3, proj3, proj3, q_gain.reshape(1, HEAD_DIM), k_gain.reshape(1, HEAD_DIM),
      cc, ss, cc, ss)


def _head_ones(width):
    r = lax.broadcasted_iota(jnp.int32, (width, width), 0) // RW_HEAD_DIM
    c = lax.broadcasted_iota(jnp.int32, (width, width), 1) // RW_HEAD_DIM
    return (r == c).astype(BF16)


def _head_sum2(x, ones):
    hi, lo = _split2(x)
    return (jnp.dot(hi, ones, preferred_element_type=F32)
            + jnp.dot(lo, ones, preferred_element_type=F32))


def _rwkv_prep_kernel(c_ref, p_ref, n_ref, mu_ref, vecs_ref, dvec_ref,
                      g1_ref, g2_ref, w1_ref, w2_ref, a1_ref, a2_ref,
                      r_o, v_o, nkk_o, g_o, bonus_o, w_o, kd_o, b_o, *, tiles_per_seq):
    i = pl.program_id(0)
    tm = c_ref.shape[0]
    first = (i % tiles_per_seq) == 0
    last = (i % tiles_per_seq) == tiles_per_seq - 1
    rows = lax.broadcasted_iota(jnp.int32, (tm, GROUP_WIDTH), 0)
    ones = _head_ones(GROUP_WIDTH)

    def shifted(j):
        sl = slice(j * GROUP_WIDTH, (j + 1) * GROUP_WIDTH)
        x = c_ref[:, sl]
        prev_row = jnp.where(first, 0.0, p_ref[7:8, sl])
        next_row = jnp.where(last, 0.0, n_ref[0:1, sl])
        prev = jnp.where(rows == 0, prev_row, pltpu.roll(x, 1, axis=0))
        nxt = jnp.where(rows == tm - 1, next_row, pltpu.roll(x, tm - 1, axis=0))
        return x + (prev - x) * mu_ref[j, 0:1, :] + (nxt - x) * mu_ref[j, 1:2, :]

    r, k, v, z = (shifted(j) for j in range(4))
    k_k, k_a, r_k = vecs_ref[0:1, :], vecs_ref[1:2, :], vecs_ref[2:3, :]
    zb = z.astype(BF16)

    gmid = jax.nn.sigmoid(jnp.dot(zb, g1_ref[...], preferred_element_type=F32))
    g_o[...] = jnp.dot(gmid.astype(BF16), g2_ref[...], preferred_element_type=F32)

    kk = k * k_k
    kk = kk * lax.rsqrt(_head_sum2(kk * kk, ones) + 1e-12)
    r_o[...] = r
    v_o[...] = v
    nkk_o[...] = -kk

    bonus = jnp.zeros_like(r)
    for d in range(2):
        w0, a0 = dvec_ref[d, 0:1, :], dvec_ref[d, 1:2, :]
        wmid = jnp.tanh(jnp.dot(zb, w1_ref[d], preferred_element_type=F32))
        wlin = w0 + jnp.dot(wmid.astype(BF16), w2_ref[d], preferred_element_type=F32)
        w_log = -jax.nn.softplus(-wlin) - 0.5
        w_o[d] = -jnp.exp(w_log)
        amid = jnp.dot(zb, a1_ref[d], preferred_element_type=F32)
        a = jax.nn.sigmoid(a0 + jnp.dot(amid.astype(BF16), a2_ref[d], preferred_element_type=F32))
        kd = k * (1.0 + (a - 1.0) * k_a)
        kd_o[d] = kd
        b_o[d] = kk * a
        bonus = bonus + _head_sum2(r * kd * r_k, ones) * v
    bonus_o[...] = bonus


def _pad_cols(w, n):
    return jnp.pad(w, [(0, 0)] * (w.ndim - 1) + [(0, n - w.shape[-1])])


def _pad_rows(w, n):
    return jnp.pad(w, [(0, 0)] * (w.ndim - 2) + [(0, n - w.shape[-2]), (0, 0)])


def _rwkv_prep(proj, seq, mu, w0, w1, w2, a0, a1, a2, g1, g2, k_k, k_a, r_k, *, tm=512):
    m = proj.shape[0]
    wide = 4 * GROUP_WIDTH
    tiles_per_seq = seq // tm
    nblk8 = m // 8
    vecs = jnp.stack([k_k, k_a, r_k.reshape(GROUP_WIDTH)]
                     + [jnp.zeros((GROUP_WIDTH,), F32)] * 5).astype(F32)
    dvec = jnp.stack([jnp.stack([w0[d], a0[d]] + [jnp.zeros((GROUP_WIDTH,), F32)] * 6)
                      for d in range(2)]).astype(F32)
    row = pl.BlockSpec((tm, GROUP_WIDTH), lambda i: (i, 0))
    row2 = pl.BlockSpec((2, tm, GROUP_WIDTH), lambda i: (0, i, 0))
    full = lambda a: pl.BlockSpec(a.shape, lambda i: (0,) * a.ndim)
    g1p = _pad_cols(g1, LORA_PAD).astype(BF16)
    g2p = _pad_rows(g2, LORA_PAD).astype(BF16)
    w1p = _pad_cols(w1, LORA_PAD).astype(BF16)
    w2p = _pad_rows(w2, LORA_PAD).astype(BF16)
    a1p = _pad_cols(a1, LORA_PAD).astype(BF16)
    a2p = _pad_rows(a2, LORA_PAD).astype(BF16)
    sds = jax.ShapeDtypeStruct
    return pl.pallas_call(
        functools.partial(_rwkv_prep_kernel, tiles_per_seq=tiles_per_seq),
        grid=(m // tm,),
        in_specs=[
            pl.BlockSpec((tm, wide), lambda i: (i, 1)),
            pl.BlockSpec((8, wide), lambda i: (jnp.maximum(i * (tm // 8) - 1, 0), 1)),
            pl.BlockSpec((8, wide), lambda i: (jnp.minimum((i + 1) * (tm // 8), nblk8 - 1), 1)),
            full(mu), full(vecs), full(dvec),
            full(g1p), full(g2p), full(w1p), full(w2p), full(a1p), full(a2p),
        ],
        out_specs=[row, row, row, row, row, row2, row2, row2],
        out_shape=[sds((m, GROUP_WIDTH), F32)] * 5 + [sds((2, m, GROUP_WIDTH), F32)] * 3,
        compiler_params=_cparams(("parallel",)),
        name="rwkv_prep",
    )(proj, proj, proj, mu, vecs, dvec, g1p, g2p, w1p, w2p, a1p, a2p)


RW_CHUNK = 64


def _bdot(a, b):
    return jnp.dot(a.astype(BF16), b.astype(BF16), preferred_element_type=F32)


def _bdot_nt(a, b):
    return lax.dot_general(a.astype(BF16), b.astype(BF16), (((1,), (1,)), ((), ())),
                           preferred_element_type=F32)


def _bdot_tn(a, b):
    return lax.dot_general(a.astype(BF16), b.astype(BF16), (((0,), (0,)), ((), ())),
                           preferred_element_type=F32)


def _rwkv_scan_kernel(rf, vf, nf, lf, kf, bf, rb, vb, nb, lb, kb, bb, yf_ref, yb_ref, st_ref,
                      *, bn):
    c = RW_CHUNK
    pw = 2 * RW_HEAD_DIM

    @pl.when(pl.program_id(0) == 0)
    def _():
        st_ref[...] = jnp.zeros_like(st_ref)

    row = lax.broadcasted_iota(jnp.int32, (c, pw), 0)
    col = lax.broadcasted_iota(jnp.int32, (c, pw), 1)
    first = col < RW_HEAD_DIM
    colh = col % RW_HEAD_DIM
    incl2 = {0: (colh <= row).astype(F32), 1: (colh >= row).astype(F32)}
    strict2 = {0: (colh < row).astype(F32), 1: (colh > row).astype(F32)}
    eye2 = (colh == row).astype(F32)
    trow = lax.broadcasted_iota(jnp.int32, (c, c), 0)
    tcol = lax.broadcasted_iota(jnp.int32, (c, c), 1)
    tri = {0: (tcol <= trow).astype(BF16), 1: (tcol >= trow).astype(BF16)}
    br = lax.broadcasted_iota(jnp.int32, (pw, pw), 0) // RW_HEAD_DIM
    bc = lax.broadcasted_iota(jnp.int32, (pw, pw), 1) // RW_HEAD_DIM
    same_head = br == bc

    def bd(x):
        return jnp.concatenate([jnp.where(first, x, 0.0), jnp.where(first, 0.0, x)], axis=0)

    npair = GROUP_WIDTH // pw
    cat = jnp.concatenate

    def body(b, carry):
        ch = []
        for d, refs in ((0, (rf, vf, nf, lf, kf, bf)), (1, (rb, vb, nb, lb, kb, bb))):
            r_, v_, a_, lw, k_, b_ = (ref[b] for ref in refs)
            hi = lw.astype(BF16)
            r1 = lw - hi.astype(F32)
            mid = r1.astype(BF16)
            lo = (r1 - mid.astype(F32)).astype(BF16)
            cs = (jnp.dot(tri[d], hi, preferred_element_type=F32)
                  + jnp.dot(tri[d], mid, preferred_element_type=F32)
                  + jnp.dot(tri[d], lo, preferred_element_type=F32))
            tot = cs[c - 1:c] if d == 0 else cs[0:1]
            g_incl = jnp.exp(cs)
            inv = jnp.exp(-cs)
            gend = jnp.exp(tot - cs)
            gtot = jnp.exp(tot)
            at = a_ * jnp.exp(cs - lw)
            bt = b_ * inv
            kt = k_ * inv
            rt = r_ * g_incl
            bh = b_ * gend
            kh = k_ * gend
            for p in range(npair):
                sl = slice(p * pw, (p + 1) * pw)
                ch.append(dict(d=d, idx=(b * 2 + d) * npair + p, vm=v_[:, sl], at=at[:, sl],
                               bt=bt[:, sl], kt=kt[:, sl], rt=rt[:, sl], bh=bh[:, sl],
                               kh=kh[:, sl], gtot=gtot[:, sl]))
        for q in ch:
            o = _bdot_nt(cat([q['at'], q['rt']], axis=0),
                         cat([bd(q['bt']), bd(q['kt'])], axis=0))
            q['a_ab'] = o[:c, :pw] * strict2[q['d']]
            q['a_ak'] = o[:c, pw:] * strict2[q['d']]
            q['q_bk'] = cat([o[c:, :pw] * incl2[q['d']], o[c:, pw:] * incl2[q['d']]], axis=1)
            q['tm'] = eye2 + q['a_ab']
        for q in ch:
            q['pp'] = _bdot(q['a_ab'], bd(q['a_ab']))
        for i in range(5):
            for q in ch:
                rr = _bdot(cat([q['tm'], q['pp']], axis=0), bd(q['pp']))
                q['tm'] = q['tm'] + rr[:c]
                q['pp'] = rr[c:]
        for q in ch:
            q['akv'] = _bdot(q['a_ak'], bd(q['vm']))
        for q in ch:
            q['w_uv'] = _bdot(q['tm'], cat([bd(q['at']), bd(q['akv'])], axis=1))
        for q in ch:
            q['s0'] = st_ref[q['idx']]
            q['uy'] = _bdot_nt(cat([q['w_uv'][:, :pw], q['rt']], axis=0), q['s0'])
        for q in ch:
            q['u'] = q['uy'][:c] + q['w_uv'][:, pw:]
            upd = _bdot_tn(cat([q['u'], q['vm']], axis=0), cat([q['bh'], q['kh']], axis=0))
            st_ref[q['idx']] = q['s0'] * q['gtot'] + jnp.where(same_head, upd, 0.0)
        for q in ch:
            q['y'] = q['uy'][c:] + _bdot(q['q_bk'], cat([bd(q['u']), bd(q['vm'])], axis=0))
        yf_ref[b] = cat([q['y'] for q in ch if q['d'] == 0], axis=1)
        yb_ref[b] = cat([q['y'] for q in ch if q['d'] == 1], axis=1)
        return carry

    lax.fori_loop(0, bn, body, 0)


def _rwkv_scan(r, v, nkk, w, kd, b):
    bn, seq, _ = r.shape
    nc = seq // RW_CHUNK
    blk = (bn, RW_CHUNK, GROUP_WIDTH)
    fwd = pl.BlockSpec(blk, lambda c: (0, c, 0))
    bwd = pl.BlockSpec(blk, lambda c: (0, nc - 1 - c, 0))
    fwd_d = pl.BlockSpec((None,) + blk, lambda c: (0, 0, c, 0))
    bwd_d = pl.BlockSpec((None,) + blk, lambda c: (1, 0, nc - 1 - c, 0))
    return pl.pallas_call(
        functools.partial(_rwkv_scan_kernel, bn=bn),
        grid=(nc,),
        in_specs=[fwd, fwd, fwd, fwd_d, fwd_d, fwd_d, bwd, bwd, bwd, bwd_d, bwd_d, bwd_d],
        out_specs=[fwd, bwd],
        out_shape=[jax.ShapeDtypeStruct((bn, seq, GROUP_WIDTH), F32)] * 2,
        scratch_shapes=[pltpu.VMEM((2 * bn * (RW_HEADS // 2), 2 * RW_HEAD_DIM, 2 * RW_HEAD_DIM), F32)],
        compiler_params=_cparams(("arbitrary",)),
        name="rwkv_scan",
    )(r, v, nkk, w, kd, b, r, v, nkk, w, kd, b)


def _rwkv_post_kernel(yf_ref, yb_ref, bonus_ref, g_ref, lnw_ref, lnb_ref, gain_ref, o_ref):
    ones = _head_ones(GROUP_WIDTH)
    y = yf_ref[...] + yb_ref[...]
    mean = _head_sum2(y, ones) * (1.0 / RW_HEAD_DIM)
    yc = y - mean
    var = _head_sum2(yc * yc, ones) * (1.0 / RW_HEAD_DIM)
    yn = yc * lax.rsqrt(var + RW_GN_EPS) * lnw_ref[...] + lnb_ref[...]
    out = (yn + bonus_ref[...]) * g_ref[...]
    o_ref[...] = _rms(out, gain_ref[...]).astype(BF16)


def _rwkv_post(yf, yb, bonus, g, ln_w, ln_b, gain, *, tm=512):
    m = yf.shape[0]
    row = pl.BlockSpec((tm, GROUP_WIDTH), lambda i: (i, 0))
    vec = pl.BlockSpec((1, GROUP_WIDTH), lambda i: (0, 0))
    v2 = lambda a: a.reshape(1, GROUP_WIDTH)
    return pl.pallas_call(
        _rwkv_post_kernel,
        grid=(m // tm,),
        in_specs=[row, row, row, row, vec, vec, vec],
        out_specs=row,
        out_shape=jax.ShapeDtypeStruct((m, GROUP_WIDTH), BF16),
        compiler_params=_cparams(("parallel",)),
        name="rwkv_post",
    )(yf, yb, bonus, g, v2(ln_w), v2(ln_b), v2(gain))


def _norm_rows_kernel(x_ref, g_ref, o_ref):
    o_ref[...] = _rms(x_ref[...], g_ref[...]).astype(BF16)


def _norm_rows(x, gain, *, tm=512):
    m, n = x.shape
    return pl.pallas_call(
        _norm_rows_kernel,
        grid=(m // tm,),
        in_specs=[pl.BlockSpec((tm, n), lambda i: (i, 0)), pl.BlockSpec((1, n), lambda i: (0, 0))],
        out_specs=pl.BlockSpec((tm, n), lambda i: (i, 0)),
        out_shape=jax.ShapeDtypeStruct((m, n), BF16),
        compiler_params=_cparams(("parallel",)),
        name="norm_rows",
    )(x, gain.reshape(1, n))


def _out_proj_kernel(ya_ref, yb_ref, yc_ref, yd_ref, w_ref, x_ref, o_ref):
    acc = x_ref[...]
    for g, y_ref in enumerate((ya_ref, yb_ref, yc_ref, yd_ref)):
        acc = acc + jnp.dot(y_ref[...], w_ref[g * GROUP_WIDTH:(g + 1) * GROUP_WIDTH, :],
                            preferred_element_type=F32)
    o_ref[...] = acc


def _out_proj(ya, yb, yc, yd, w_out, x, *, tm=512, tn=1024):
    m, n = x.shape
    row = pl.BlockSpec((tm, GROUP_WIDTH), lambda i, j: (i, 0))
    return pl.pallas_call(
        _out_proj_kernel,
        grid=(m // tm, n // tn),
        in_specs=[row, row, row, row,
                  pl.BlockSpec((N_MIXERS * GROUP_WIDTH, tn), lambda i, j: (0, j)),
                  pl.BlockSpec((tm, tn), lambda i, j: (i, j))],
        out_specs=pl.BlockSpec((tm, tn), lambda i, j: (i, j)),
        out_shape=jax.ShapeDtypeStruct((m, n), F32),
        compiler_params=_cparams(("parallel", "parallel")),
        name="out_proj",
    )(ya, yb, yc, yd, w_out.astype(BF16), x)


PEER_TOK = 256
NEG_INF = float("-inf")


def _peer_stats_kernel(q_ref, khi_ref, klo_ref, th_o, e1_o, sc2_o, e2_o,
                       sc1_ref, sc2_ref, top_ref, cand_ref):
    nt = q_ref.shape[0]
    for h in range(PEER_HEADS):
        qh = q_ref[:, h * LANES:(h + 1) * LANES]
        qhi, qlo = _split2(qh)
        nt_dot = lambda a, b: lax.dot_general(a, b, (((1,), (1,)), ((), ())),
                                              preferred_element_type=F32)
        for p in range(2):
            sc = (nt_dot(khi_ref[h, p], qhi) + nt_dot(khi_ref[h, p], qlo)
                  + nt_dot(klo_ref[h, p], qhi))
            if p == 0:
                sc1_ref[...] = sc
            else:
                sc2_ref[...] = sc
            x = sc
            for i in range(PEER_TOPK):
                m = jnp.max(x, axis=0, keepdims=True)
                top_ref[p, i:i + 1, :] = m
                x = jnp.where(x == m, NEG_INF, x)
        s2 = top_ref[1]
        for i in range(PEER_TOPK):
            cand_ref[i * PEER_TOPK:(i + 1) * PEER_TOPK, :] = top_ref[0, i:i + 1, :] + s2
        x = cand_ref[...]
        top = top_ref[0, 0:1, :] + top_ref[1, 0:1, :]
        zsum = jnp.zeros((1, nt), F32)
        m = top
        for i in range(PEER_TOPK):
            m = jnp.max(x, axis=0, keepdims=True)
            zsum = zsum + jnp.exp(m - top)
            x = jnp.where(x == m, NEG_INF, x)
        thr = m
        sc1 = sc1_ref[...]
        th = jnp.full(sc1.shape, jnp.inf, F32)
        for i in range(PEER_TOPK):
            s1i = top_ref[0, i:i + 1, :]
            row_th = jnp.min(jnp.where(s1i + s2 >= thr, s2, jnp.inf), axis=0, keepdims=True)
            th = jnp.where(sc1 == s1i, row_th, th)
        sc2 = sc2_ref[...]
        e1 = jnp.exp(sc1 - top_ref[0, 0:1, :]) / zsum
        e2 = jnp.exp(sc2 - top_ref[1, 0:1, :])
        for t in range(nt // LANES):
            tl = slice(t * LANES, (t + 1) * LANES)
            th_o[h, t] = th[:, tl]
            e1_o[h, t] = e1[:, tl]
            sc2_o[h, t] = sc2[:, tl]
            e2_o[h, t] = e2[:, tl]


def _peer_stats(q, sub_keys):
    ntok = q.shape[0]
    keys = jnp.zeros((PEER_HEADS, 2, PEER_N_KEYS, LANES), F32)
    keys = keys.at[:, 0, :, :PEER_HALF].set(sub_keys[:, 0]).at[:, 1, :, PEER_HALF:].set(sub_keys[:, 1])
    khi = keys.astype(BF16)
    klo = (keys - khi.astype(F32)).astype(BF16)
    kspec = pl.BlockSpec(keys.shape, lambda i: (0, 0, 0, 0))
    big = pl.BlockSpec((PEER_HEADS, PEER_TOK // LANES, PEER_N_KEYS, LANES), lambda i: (0, i, 0, 0))
    return pl.pallas_call(
        _peer_stats_kernel,
        grid=(ntok // PEER_TOK,),
        in_specs=[pl.BlockSpec((PEER_TOK, PEER_HEADS * LANES), lambda i: (i, 0)), kspec, kspec],
        out_specs=[big] * 4,
        out_shape=[jax.ShapeDtypeStruct((PEER_HEADS, ntok // LANES, PEER_N_KEYS, LANES), F32)] * 4,
        scratch_shapes=[pltpu.VMEM((PEER_N_KEYS, PEER_TOK), F32),
                        pltpu.VMEM((PEER_N_KEYS, PEER_TOK), F32),
                        pltpu.VMEM((2, PEER_TOPK, PEER_TOK), F32),
                        pltpu.VMEM((PEER_TOPK * PEER_TOPK, PEER_TOK), F32)],
        compiler_params=_cparams(("parallel",)),
        name="peer_stats",
    )(q, khi, klo)


PEER_TB = 512
PEER_ET = 512
PEER_ROWS = 32


def _peer_expert_kernel(hn_ref, u_ref, v_ref, th_ref, e1_ref, sc2_ref, e2_ref, x_ref, o_ref,
                        g_ref, w_ref):
    j = pl.program_id(1)
    last = pl.num_programs(1) - 2
    na = PEER_ET // PEER_N_KEYS
    slot = j % 2

    @pl.when(j == 0)
    def _():
        o_ref[...] = x_ref[...]
        w_ref[1] = jnp.zeros(w_ref.shape[1:], BF16)

    heads = range(PEER_HEADS)
    a0 = jnp.minimum(j, last) * na
    for t in range(PEER_TB // LANES):
        tl = slice(t * LANES, (t + 1) * LANES)
        for rb in range(PEER_N_KEYS // PEER_ROWS):
            bs = slice(rb * PEER_ROWS, (rb + 1) * PEER_ROWS)
            gate = [jnp.zeros((PEER_ROWS, LANES), F32) for _ in range(na)]
            for h in heads:
                s2 = sc2_ref[h, t, bs, :]
                x2 = e2_ref[h, t, bs, :]
                for ai in range(na):
                    th = th_ref[h, t, pl.ds(a0 + ai, 1), :]
                    e1 = e1_ref[h, t, pl.ds(a0 + ai, 1), :]
                    gate[ai] = gate[ai] + jnp.where(s2 >= th, x2, 0.0) * e1
            for ai in range(na):
                rows = slice(ai * PEER_N_KEYS + rb * PEER_ROWS, ai * PEER_N_KEYS + (rb + 1) * PEER_ROWS)
                g_ref[rows, tl] = gate[ai]

    o_ref[...] += lax.dot_general(w_ref[1 - slot], v_ref[...], (((0,), (0,)), ((), ())),
                                  preferred_element_type=F32)

    act = lax.dot_general(u_ref[...], hn_ref[...], (((1,), (1,)), ((), ())),
                          preferred_element_type=F32)
    gel = 0.5 * act * (1.0 + lax.erf(act * (1.0 / math.sqrt(2.0))))
    w_ref[slot] = (g_ref[...] * gel).astype(BF16)


def _peer_experts(hn, u, v, th, e1, sc2, e2, x):
    ntok, d = x.shape
    nexp = u.shape[0]
    ntile = nexp // PEER_ET
    stat = pl.BlockSpec((PEER_HEADS, PEER_TB // LANES, PEER_N_KEYS, LANES), lambda i, j: (0, i, 0, 0))
    tok = pl.BlockSpec((PEER_TB, d), lambda i, j: (i, 0))
    u_tile = pl.BlockSpec((PEER_ET, d), lambda i, j: (jnp.minimum(j, ntile - 1), 0))
    v_tile = pl.BlockSpec((PEER_ET, d), lambda i, j: (jnp.maximum(j - 1, 0), 0))
    return pl.pallas_call(
        _peer_expert_kernel,
        grid=(ntok // PEER_TB, ntile + 1),
        in_specs=[tok, u_tile, v_tile, stat, stat, stat, stat, tok],
        out_specs=tok,
        out_shape=jax.ShapeDtypeStruct((ntok, d), F32),
        scratch_shapes=[pltpu.VMEM((PEER_ET, PEER_TB), F32), pltpu.VMEM((2, PEER_ET, PEER_TB), BF16)],
        compiler_params=_cparams(("parallel", "arbitrary")),
        name="peer_experts",
    )(hn, u, v, th, e1, sc2, e2, x)


def _peer_ffn(x, norm_gain, w_query, sub_keys, expert_u, expert_v):
    q, hn = _norm_matmul(x, norm_gain, w_query.astype(BF16))
    th, e1, sc2, e2 = _peer_stats(q, sub_keys)
    return _peer_experts(hn, expert_u.astype(BF16), expert_v.astype(BF16), th, e1, sc2, e2, x)


def kernel(x, norm_mix, w_in, fourier_w, s5_lam_re, s5_lam_im, s5_log_step, s5_b_re, s5_b_im, s5_c_re, s5_c_im, s5_d, s5_w_glu, attn_q_norm, attn_k_norm, rwkv_mu, rwkv_w0, rwkv_w1, rwkv_w2, rwkv_a0, rwkv_a1, rwkv_a2, rwkv_g1, rwkv_g2, rwkv_k_k, rwkv_k_a, rwkv_r_k, rwkv_ln_w, rwkv_ln_b, branch_norm, w_out, norm_ffn, peer_w_query, peer_sub_keys, peer_u, peer_v):
    bn, seq, d = x.shape
    m = bn * seq
    depth = w_in.shape[0]
    dft_s, dft_c = _dft_tables(seq)
    cc, ss = _rope_tables(seq)
    xf = x.reshape(m, d)
    for l in range(depth):
        bgain = branch_norm[l].reshape(N_MIXERS, GROUP_WIDTH)
        proj, _ = _norm_matmul(xf, norm_mix[l], w_in[l].astype(BF16))
        proj3 = proj.reshape(bn, seq, PROJ_WIDTH)

        ya = _fnet_mix(proj3, dft_s, dft_c, fourier_w[l], bgain[0]).reshape(m, GROUP_WIDTH)

        b_blk, c_blk, lam1, pw = _s5_params(s5_lam_re[l], s5_lam_im[l], s5_log_step[l],
                                            s5_b_re[l], s5_b_im[l], s5_c_re[l], s5_c_im[l])
        u_s5 = _s5_row_order(proj3[:, :, GROUP_WIDTH:2 * GROUP_WIDTH])
        yf, yb = _s5_scan(u_s5, b_blk, c_blk, lam1, pw)
        ybm = _s5_tail(yf.reshape(m, GROUP_WIDTH), yb.reshape(m, GROUP_WIDTH),
                       u_s5.reshape(m, GROUP_WIDTH), s5_d[l], s5_w_glu[l], bgain[1])
        ybm = _s5_row_order(ybm.reshape(bn, seq, GROUP_WIDTH), inverse=True).reshape(m, GROUP_WIDTH)

        att = _attention_mix(proj3, attn_q_norm[l], attn_k_norm[l], cc, ss)
        yc = _norm_rows(att.reshape(m, GROUP_WIDTH), bgain[2])

        r, v, nkk, g, bonus, w, kd, b = _rwkv_prep(
            proj, seq, rwkv_mu[l], rwkv_w0[l], rwkv_w1[l], rwkv_w2[l], rwkv_a0[l], rwkv_a1[l],
            rwkv_a2[l], rwkv_g1[l], rwkv_g2[l], rwkv_k_k[l], rwkv_k_a[l], rwkv_r_k[l])
        s3 = lambda a: a.reshape(bn, seq, GROUP_WIDTH)
        s4 = lambda a: a.reshape(2, bn, seq, GROUP_WIDTH)
        y_f, y_b = _rwkv_scan(s3(r), s3(v), s3(nkk), s4(w), s4(kd), s4(b))
        yd = _rwkv_post(y_f.reshape(m, GROUP_WIDTH), y_b.reshape(m, GROUP_WIDTH), bonus, g,
                        rwkv_ln_w[l], rwkv_ln_b[l], bgain[3])

        xf = _out_proj(ya, ybm, yc, yd, w_out[l], xf)
        xf = _peer_ffn(xf, norm_ffn[l], peer_w_query[l], peer_sub_keys[l], peer_u[l], peer_v[l])
    return xf.reshape(bn, seq, d)
```

```python
import functools
import math

import jax
import jax.numpy as jnp
import numpy as np
from jax import lax
from jax.experimental import pallas as pl
from jax.experimental.pallas import tpu as pltpu

F32 = jnp.float32
BF16 = jnp.bfloat16

D_MODEL = 2048
GROUP_WIDTH = 512
N_MIXERS = 4
FN_HEADS = 4
FN_HEAD_DIM = 128
S5_GROUP_CH = 16
S5_GROUPS = 32
S5_STATE = 64
S5_NSTATE = S5_GROUPS * S5_STATE
HEAD_DIM = 128
ATT_HEADS = 4
KV_HEADS = 2
GRID_W = 64
ROPE_THETA = 10000.0
AXIS_ROT_DIM = 64
RW_HEAD_DIM = 64
RW_HEADS = 8
RW_GN_EPS = 64e-5
LORA_PAD = 128
PEER_HEADS = 8
PEER_HALF = 64
PEER_N_KEYS = 128
PEER_TOPK = 16
NORM_EPS = 1e-6
PROJ_WIDTH = 4096

LANES = 128
VMEM_LIMIT = 56 * 1024 * 1024


def _cparams(sem):
    return pltpu.CompilerParams(dimension_semantics=sem, vmem_limit_bytes=VMEM_LIMIT)


def _rms(x, gain):
    return x * lax.rsqrt(jnp.mean(x * x, axis=-1, keepdims=True) + NORM_EPS) * gain


def _split2(x):
    hi = x.astype(BF16)
    lo = (x - hi.astype(F32)).astype(BF16)
    return hi, lo


def _norm_matmul_kernel(x_ref, g_ref, w_ref, o_ref, xn_out_ref, xn_ref):
    @pl.when(pl.program_id(1) == 0)
    def _():
        xn = _rms(x_ref[...], g_ref[...]).astype(BF16)
        xn_ref[...] = xn
        xn_out_ref[...] = xn

    o_ref[...] = jnp.dot(xn_ref[...], w_ref[...], preferred_element_type=F32)


def _norm_matmul(x, gain, w, *, tm=512, tn=1024):
    m, k = x.shape
    n = w.shape[1]
    return pl.pallas_call(
        _norm_matmul_kernel,
        grid=(m // tm, n // tn),
        in_specs=[
            pl.BlockSpec((tm, k), lambda i, j: (i, 0)),
            pl.BlockSpec((1, k), lambda i, j: (0, 0)),
            pl.BlockSpec((k, tn), lambda i, j: (0, j)),
        ],
        out_specs=[
            pl.BlockSpec((tm, tn), lambda i, j: (i, j)),
            pl.BlockSpec((tm, k), lambda i, j: (i, 0)),
        ],
        out_shape=[
            jax.ShapeDtypeStruct((m, n), F32),
            jax.ShapeDtypeStruct((m, k), BF16),
        ],
        scratch_shapes=[pltpu.VMEM((tm, k), BF16)],
        compiler_params=_cparams(("parallel", "arbitrary")),
        name="norm_matmul",
    )(x, gain.reshape(1, k), w)


def _fnet_kernel(pa_ref, dftc_ref, dfts_ref, wf_ref, g_ref, o_ref, z_ref, *, seq):
    @pl.when(pl.program_id(1) == 0)
    def _():
        for h in range(FN_HEADS):
            sl = slice(h * FN_HEAD_DIM, (h + 1) * FN_HEAD_DIM)
            xh = pa_ref[0, :, sl].astype(BF16)
            t = jnp.dot(xh, dftc_ref[...], preferred_element_type=F32)
            z_ref[0:seq, sl] = t[:, :FN_HEAD_DIM].astype(BF16)
            z_ref[seq:2 * seq, sl] = t[:, FN_HEAD_DIM:].astype(BF16)

    scale = 1.0 / math.sqrt(seq * FN_HEAD_DIM)
    re = jnp.dot(dfts_ref[...], z_ref[...], preferred_element_type=F32) * scale
    ya = jnp.dot(re.astype(BF16), wf_ref[...], preferred_element_type=F32)
    o_ref[0] = _rms(ya, g_ref[...]).astype(BF16)


def _dft_tables(seq):
    def cos_sin(n):
        k = np.arange(n, dtype=np.int64)
        ang = ((k[:, None] * k[None, :]) % n).astype(np.float64) * (2.0 * math.pi / n)
        return np.cos(ang), np.sin(ang)

    cs, sn = cos_sin(seq)
    dft_s = jnp.asarray(np.concatenate([cs, -sn], axis=1).astype(np.float32)).astype(BF16)
    cs, sn = cos_sin(FN_HEAD_DIM)
    dft_c = jnp.asarray(np.concatenate([cs, sn], axis=1).astype(np.float32)).astype(BF16)
    return dft_s, dft_c


def _fnet_mix(proj3, dft_s, dft_c, w_f, gain, *, tm=512):
    bn, seq, _ = proj3.shape
    return pl.pallas_call(
        functools.partial(_fnet_kernel, seq=seq),
        grid=(bn, seq // tm),
        in_specs=[
            pl.BlockSpec((1, seq, GROUP_WIDTH), lambda b, r: (b, 0, 0)),
            pl.BlockSpec((FN_HEAD_DIM, 2 * FN_HEAD_DIM), lambda b, r: (0, 0)),
            pl.BlockSpec((tm, 2 * seq), lambda b, r: (r, 0)),
            pl.BlockSpec((GROUP_WIDTH, GROUP_WIDTH), lambda b, r: (0, 0)),
            pl.BlockSpec((1, GROUP_WIDTH), lambda b, r: (0, 0)),
        ],
        out_specs=pl.BlockSpec((1, tm, GROUP_WIDTH), lambda b, r: (b, r, 0)),
        out_shape=jax.ShapeDtypeStruct((bn, seq, GROUP_WIDTH), BF16),
        scratch_shapes=[pltpu.VMEM((2 * seq, GROUP_WIDTH), BF16)],
        compiler_params=_cparams(("parallel", "arbitrary")),
        name="fnet_mix",
    )(proj3, dft_c, dft_s, w_f.astype(BF16), gain.reshape(1, GROUP_WIDTH))


S5_CHUNK = 256
S5_SEGS = 8
S5_STEPS = S5_CHUNK // S5_SEGS
S5_TILES_PER_PASS = 4


def _s5_params(lam_re, lam_im, log_step, b_re, b_im, c_re, c_im):
    step = jnp.exp(log_step.astype(F32))[..., None]
    lr = lam_re.astype(F32)
    li = lam_im.astype(F32)
    ar = lr * step
    ai = li * step
    mag = jnp.exp(ar)
    lbr = mag * jnp.cos(ai)
    lbi = mag * jnp.sin(ai)
    den = lr * lr + li * li
    qr = ((lbr - 1.0) * lr + lbi * li) / den
    qi = (lbi * lr - (lbr - 1.0) * li) / den
    bbr = qr[..., None] * b_re - qi[..., None] * b_im
    bbi = qr[..., None] * b_im + qi[..., None] * b_re
    eye = jnp.eye(S5_GROUPS, dtype=F32)

    def in_blk(b):
        t = jnp.einsum('dgph,gk->dghkp', b, eye)
        return t.reshape(2, GROUP_WIDTH, S5_NSTATE)

    def out_blk(c):
        t = jnp.einsum('dghp,gk->dgpkh', c, eye)
        return t.reshape(2, S5_NSTATE, GROUP_WIDTH)

    b_blk = jnp.concatenate([in_blk(bbr), in_blk(bbi)], axis=2).astype(BF16)
    c_blk = jnp.concatenate([out_blk(c_re.astype(F32)), -out_blk(c_im.astype(F32))], axis=1).astype(BF16)
    kpow = jnp.arange(1, S5_STEPS + 1, dtype=F32)[None, :, None, None]
    pmag = jnp.exp(ar[:, None] * kpow)
    pang = ai[:, None] * kpow
    pw = jnp.concatenate([(pmag * jnp.cos(pang)).reshape(2, S5_STEPS, S5_NSTATE),
                          (pmag * jnp.sin(pang)).reshape(2, S5_STEPS, S5_NSTATE)], axis=2)
    lam1 = pw[:, 0:1, :]
    pw_tiles = pw.reshape(2, S5_STEPS, 2 * S5_NSTATE // LANES, LANES).transpose(0, 2, 1, 3)
    return b_blk, c_blk, lam1, pw_tiles


def _s5_scan_kernel(uf_ref, ub_ref, bblk_ref, cblk_ref, lam_ref, pw_ref, yf_ref, yb_ref,
                    bu_ref, carry_ref, cin_ref):
    ns = S5_NSTATE

    @pl.when(pl.program_id(1) == 0)
    def _():
        carry_ref[...] = jnp.zeros_like(carry_ref)

    nt = ns // LANES
    for d, u_ref in ((0, uf_ref), (1, ub_ref)):
        bu = jnp.dot(u_ref[0].astype(BF16), bblk_ref[d], preferred_element_type=F32)
        for q in range(2 * nt):
            bu_ref[d, q] = bu[:, q * LANES:(q + 1) * LANES]

    def lanes(q):
        return slice(q * LANES, (q + 1) * LANES)

    for q0 in range(0, nt, S5_TILES_PER_PASS):
        tiles = range(q0, q0 + S5_TILES_PER_PASS)
        lam = {(d, q): (lam_ref[d, :, lanes(q)], lam_ref[d, :, lanes(nt + q)])
               for d in range(2) for q in tiles}

        def step(i, hs, tiles=tiles, lam=lam):
            out = []
            for d in range(2):
                row = i if d == 0 else S5_STEPS - 1 - i
                idx = pl.ds(pl.multiple_of(row * S5_SEGS, S5_SEGS), S5_SEGS)
                for q in tiles:
                    hr, hi = hs[len(out)], hs[len(out) + 1]
                    lr, li = lam[d, q]
                    nr = lr * hr - li * hi + bu_ref[d, q, idx, :]
                    ni = lr * hi + li * hr + bu_ref[d, nt + q, idx, :]
                    bu_ref[d, q, idx, :] = nr
                    bu_ref[d, nt + q, idx, :] = ni
                    out += [nr, ni]
            return tuple(out)

        z = jnp.zeros((S5_SEGS, LANES), F32)
        ends = lax.fori_loop(0, S5_STEPS, step, (z,) * (4 * S5_TILES_PER_PASS))

        cin = {}
        pos = 0
        for d in range(2):
            order = range(S5_SEGS) if d == 0 else range(S5_SEGS - 1, -1, -1)
            for q in tiles:
                er, ei = ends[pos], ends[pos + 1]
                pos += 2
                pr = pw_ref[d, q, S5_STEPS - 1:S5_STEPS, :]
                pi = pw_ref[d, nt + q, S5_STEPS - 1:S5_STEPS, :]
                cr = carry_ref[d, :, lanes(q)]
                ci = carry_ref[d, :, lanes(nt + q)]
                for sg in order:
                    cin_ref[d, sg:sg + 1, lanes(q)] = cr
                    cin_ref[d, sg:sg + 1, lanes(nt + q)] = ci
                    nr = er[sg:sg + 1] + pr * cr - pi * ci
                    ni = ei[sg:sg + 1] + pr * ci + pi * cr
                    cr, ci = nr, ni
                carry_ref[d, :, lanes(q)] = cr
                carry_ref[d, :, lanes(nt + q)] = ci
                cin[d, q] = (cin_ref[d, :, lanes(q)], cin_ref[d, :, lanes(nt + q)])

        def fix(i, c, tiles=tiles, cin=cin):
            for d in range(2):
                row = i if d == 0 else S5_STEPS - 1 - i
                idx = pl.ds(pl.multiple_of(row * S5_SEGS, S5_SEGS), S5_SEGS)
                for q in tiles:
                    pr = pw_ref[d, q, pl.ds(i, 1), :]
                    pi = pw_ref[d, nt + q, pl.ds(i, 1), :]
                    cr, ci = cin[d, q]
                    bu_ref[d, q, idx, :] = bu_ref[d, q, idx, :] + pr * cr - pi * ci
                    bu_ref[d, nt + q, idx, :] = bu_ref[d, nt + q, idx, :] + pr * ci + pi * cr
            return c

        lax.fori_loop(0, S5_STEPS, fix, 0)

    for d, y_ref in ((0, yf_ref), (1, yb_ref)):
        hs = jnp.concatenate([bu_ref[d, q].astype(BF16) for q in range(2 * nt)], axis=1)
        y_ref[0] = jnp.dot(hs, cblk_ref[d], preferred_element_type=F32)


def _s5_row_order(x, inverse=False):
    bn, seq, n = x.shape
    a, b = (S5_STEPS, S5_SEGS) if inverse else (S5_SEGS, S5_STEPS)
    return x.reshape(bn, seq // S5_CHUNK, a, b, n).transpose(0, 1, 3, 2, 4).reshape(bn, seq, n)


def _s5_scan(u, b_blk, c_blk, lam1, pw):
    bn, seq, _ = u.shape
    nc = seq // S5_CHUNK
    ns2 = 2 * S5_NSTATE
    u_blk = (1, S5_CHUNK, GROUP_WIDTH)
    return pl.pallas_call(
        _s5_scan_kernel,
        grid=(bn, nc),
        in_specs=[
            pl.BlockSpec(u_blk, lambda b, c: (b, c, 0)),
            pl.BlockSpec(u_blk, lambda b, c: (b, nc - 1 - c, 0)),
            pl.BlockSpec((2, GROUP_WIDTH, ns2), lambda b, c: (0, 0, 0)),
            pl.BlockSpec((2, ns2, GROUP_WIDTH), lambda b, c: (0, 0, 0)),
            pl.BlockSpec((2, 1, ns2), lambda b, c: (0, 0, 0)),
            pl.BlockSpec((2, ns2 // LANES, S5_STEPS, LANES), lambda b, c: (0, 0, 0, 0)),
        ],
        out_specs=[
            pl.BlockSpec(u_blk, lambda b, c: (b, c, 0)),
            pl.BlockSpec(u_blk, lambda b, c: (b, nc - 1 - c, 0)),
        ],
        out_shape=[jax.ShapeDtypeStruct((bn, seq, GROUP_WIDTH), F32)] * 2,
        scratch_shapes=[
            pltpu.VMEM((2, ns2 // LANES, S5_CHUNK, LANES), F32),
            pltpu.VMEM((2, 1, ns2), F32),
            pltpu.VMEM((2, S5_SEGS, ns2), F32),
        ],
        compiler_params=_cparams(("parallel", "arbitrary")),
        name="s5_scan",
    )(u, u, b_blk, c_blk, lam1, pw)


def _s5_tail_kernel(yf_ref, yb_ref, u_ref, d_ref, wg_ref, g_ref, o_ref):
    y = yf_ref[...] + yb_ref[...] + d_ref[...] * u_ref[...]
    y = jax.nn.gelu(y)
    gate = jnp.dot(y.astype(BF16), wg_ref[...], preferred_element_type=F32)
    o_ref[...] = _rms(y * jax.nn.sigmoid(gate), g_ref[...]).astype(BF16)


def _s5_tail(yf, yb, u, d_skip, w_glu, gain, *, tm=512):
    m = yf.shape[0]
    row = pl.BlockSpec((tm, GROUP_WIDTH), lambda i: (i, 0))
    vec = pl.BlockSpec((1, GROUP_WIDTH), lambda i: (0, 0))
    return pl.pallas_call(
        _s5_tail_kernel,
        grid=(m // tm,),
        in_specs=[row, row, row, vec,
                  pl.BlockSpec((GROUP_WIDTH, GROUP_WIDTH), lambda i: (0, 0)), vec],
        out_specs=row,
        out_shape=jax.ShapeDtypeStruct((m, GROUP_WIDTH), BF16),
        compiler_params=_cparams(("parallel",)),
        name="s5_tail",
    )(yf, yb, u, d_skip.reshape(1, GROUP_WIDTH), w_glu.astype(BF16), gain.reshape(1, GROUP_WIDTH))


def _rope(x, cc, ss):
    return x * cc + pltpu.roll(x, HEAD_DIM // 2, axis=1) * ss


def _attn_kernel(q_ref, k_ref, v_ref, qg_ref, kg_ref, ccq_ref, ssq_ref, cck_ref, ssk_ref,
                 o_ref, kp_ref, vp_ref):
    @pl.when(pl.program_id(2) == 0)
    def _():
        kn = _rms(k_ref[0], kg_ref[...])
        kp_ref[...] = _rope(kn, cck_ref[...], ssk_ref[...]).astype(BF16)
        vp_ref[...] = v_ref[0].astype(BF16)

    rep = ATT_HEADS // KV_HEADS
    outs = []
    for r in range(rep):
        q = q_ref[0, :, r * HEAD_DIM:(r + 1) * HEAD_DIM]
        qn = _rope(_rms(q, qg_ref[...]), ccq_ref[...], ssq_ref[...]) * (HEAD_DIM ** -0.5)
        s = lax.dot_general(qn.astype(BF16), kp_ref[...], (((1,), (1,)), ((), ())),
                            preferred_element_type=F32)
        p = jnp.exp(s - jnp.max(s, axis=-1, keepdims=True))
        inv = 1.0 / jnp.sum(p, axis=-1, keepdims=True)
        o = jnp.dot(p.astype(BF16), vp_ref[...], preferred_element_type=F32) * inv
        outs.append(o)
    o_ref[0] = jnp.concatenate(outs, axis=1)


def _rope_tables(seq):
    rows = seq // GRID_W
    row_id = jnp.repeat(jnp.arange(rows), GRID_W).astype(F32)
    col_id = jnp.tile(jnp.arange(GRID_W), rows).astype(F32)
    inv = ROPE_THETA ** (-jnp.arange(0, AXIS_ROT_DIM, 2, dtype=F32) / AXIS_ROT_DIM)
    ang = jnp.concatenate([row_id[:, None] * inv, col_id[:, None] * inv], axis=-1)
    cos, sin = jnp.cos(ang), jnp.sin(ang)
    return jnp.concatenate([cos, cos], axis=1), jnp.concatenate([-sin, sin], axis=1)


def _attention_mix(proj3, q_gain, k_gain, cc, ss, *, tq=256):
    bn, seq, _ = proj3.shape
    rep = ATT_HEADS // KV_HEADS
    qw = rep * HEAD_DIM
    q_col0 = (2 * GROUP_WIDTH) // qw
    k_col0 = (3 * GROUP_WIDTH) // HEAD_DIM
    v_col0 = k_col0 + KV_HEADS
    vec = pl.BlockSpec((1, HEAD_DIM), lambda b, g, i: (0, 0))
    return pl.pallas_call(
        _attn_kernel,
        grid=(bn, KV_HEADS, seq // tq),
        in_specs=[
            pl.BlockSpec((1, tq, qw), lambda b, g, i: (b, i, q_col0 + g)),
            pl.BlockSpec((1, seq, HEAD_DIM), lambda b, g, i: (b, 0, k_col0 + g)),
            pl.BlockSpec((1, seq, HEAD_DIM), lambda b, g, i: (b, 0, v_col0 + g)),
            vec, vec,
            pl.BlockSpec((tq, HEAD_DIM), lambda b, g, i: (i, 0)),
            pl.BlockSpec((tq, HEAD_DIM), lambda b, g, i: (i, 0)),
            pl.BlockSpec((seq, HEAD_DIM), lambda b, g, i: (0, 0)),
            pl.BlockSpec((seq, HEAD_DIM), lambda b, g, i: (0, 0)),
        ],
        out_specs=pl.BlockSpec((1, tq, qw), lambda b, g, i: (b, i, g)),
        out_shape=jax.ShapeDtypeStruct((bn, seq, GROUP_WIDTH), F32),
        scratch_shapes=[pltpu.VMEM((seq, HEAD_DIM), BF16), pltpu.VMEM((seq, HEAD_DIM), BF16)],
        compiler_params=_cparams(("parallel", "parallel", "arbitrary")),
        name="gqa_attention",
    )(proj3, proj3, proj3, q_gain.reshape(1, HEAD_DIM), k_gain.reshape(1, HEAD_DIM),
      cc, ss, cc, ss)


def _head_ones(width):
    r = lax.broadcasted_iota(jnp.int32, (width, width), 0) // RW_HEAD_DIM
    c = lax.broadcasted_iota(jnp.int32, (width, width), 1) // RW_HEAD_DIM
    return (r == c).astype(BF16)


def _head_sum2(x, ones):
    hi, lo = _split2(x)
    return (jnp.dot(hi, ones, preferred_element_type=F32)
            + jnp.dot(lo, ones, preferred_element_type=F32))


def _rwkv_prep_kernel(c_ref, p_ref, n_ref, mu_ref, vecs_ref, dvec_ref,
                      g1_ref, g2_ref, w1_ref, w2_ref, a1_ref, a2_ref,
                      r_o, v_o, nkk_o, g_o, bonus_o, w_o, kd_o, b_o, *, tiles_per_seq):
    i = pl.program_id(0)
    tm = c_ref.shape[0]
    first = (i % tiles_per_seq) == 0
    last = (i % tiles_per_seq) == tiles_per_seq - 1
    rows = lax.broadcasted_iota(jnp.int32, (tm, GROUP_WIDTH), 0)
    ones = _head_ones(GROUP_WIDTH)

    def shifted(j):
        sl = slice(j * GROUP_WIDTH, (j + 1) * GROUP_WIDTH)
        x = c_ref[:, sl]
        prev_row = jnp.where(first, 0.0, p_ref[7:8, sl])
        next_row = jnp.where(last, 0.0, n_ref[0:1, sl])
        prev = jnp.where(rows == 0, prev_row, pltpu.roll(x, 1, axis=0))
        nxt = jnp.where(rows == tm - 1, next_row, pltpu.roll(x, tm - 1, axis=0))
        return x + (prev - x) * mu_ref[j, 0:1, :] + (nxt - x) * mu_ref[j, 1:2, :]

    r, k, v, z = (shifted(j) for j in range(4))
    k_k, k_a, r_k = vecs_ref[0:1, :], vecs_ref[1:2, :], vecs_ref[2:3, :]
    zb = z.astype(BF16)

    gmid = jax.nn.sigmoid(jnp.dot(zb, g1_ref[...], preferred_element_type=F32))
    g_o[...] = jnp.dot(gmid.astype(BF16), g2_ref[...], preferred_element_type=F32)

    kk = k * k_k
    kk = kk * lax.rsqrt(_head_sum2(kk * kk, ones) + 1e-12)
    r_o[...] = r
    v_o[...] = v
    nkk_o[...] = -kk

    bonus = jnp.zeros_like(r)
    for d in range(2):
        w0, a0 = dvec_ref[d, 0:1, :], dvec_ref[d, 1:2, :]
        wmid = jnp.tanh(jnp.dot(zb, w1_ref[d], preferred_element_type=F32))
        wlin = w0 + jnp.dot(wmid.astype(BF16), w2_ref[d], preferred_element_type=F32)
        w_log = -jax.nn.softplus(-wlin) - 0.5
        w_o[d] = -jnp.exp(w_log)
        amid = jnp.dot(zb, a1_ref[d], preferred_element_type=F32)
        a = jax.nn.sigmoid(a0 + jnp.dot(amid.astype(BF16), a2_ref[d], preferred_element_type=F32))
        kd = k * (1.0 + (a - 1.0) * k_a)
        kd_o[d] = kd
        b_o[d] = kk * a
        bonus = bonus + _head_sum2(r * kd * r_k, ones) * v
    bonus_o[...] = bonus


def _pad_cols(w, n):
    return jnp.pad(w, [(0, 0)] * (w.ndim - 1) + [(0, n - w.shape[-1])])


def _pad_rows(w, n):
    return jnp.pad(w, [(0, 0)] * (w.ndim - 2) + [(0, n - w.shape[-2]), (0, 0)])


def _rwkv_prep(proj, seq, mu, w0, w1, w2, a0, a1, a2, g1, g2, k_k, k_a, r_k, *, tm=512):
    m = proj.shape[0]
    wide = 4 * GROUP_WIDTH
    tiles_per_seq = seq // tm
    nblk8 = m // 8
    vecs = jnp.stack([k_k, k_a, r_k.reshape(GROUP_WIDTH)]
                     + [jnp.zeros((GROUP_WIDTH,), F32)] * 5).astype(F32)
    dvec = jnp.stack([jnp.stack([w0[d], a0[d]] + [jnp.zeros((GROUP_WIDTH,), F32)] * 6)
                      for d in range(2)]).astype(F32)
    row = pl.BlockSpec((tm, GROUP_WIDTH), lambda i: (i, 0))
    row2 = pl.BlockSpec((2, tm, GROUP_WIDTH), lambda i: (0, i, 0))
    full = lambda a: pl.BlockSpec(a.shape, lambda i: (0,) * a.ndim)
    g1p = _pad_cols(g1, LORA_PAD).astype(BF16)
    g2p = _pad_rows(g2, LORA_PAD).astype(BF16)
    w1p = _pad_cols(w1, LORA_PAD).astype(BF16)
    w2p = _pad_rows(w2, LORA_PAD).astype(BF16)
    a1p = _pad_cols(a1, LORA_PAD).astype(BF16)
    a2p = _pad_rows(a2, LORA_PAD).astype(BF16)
    sds = jax.ShapeDtypeStruct
    return pl.pallas_call(
        functools.partial(_rwkv_prep_kernel, tiles_per_seq=tiles_per_seq),
        grid=(m // tm,),
        in_specs=[
            pl.BlockSpec((tm, wide), lambda i: (i, 1)),
            pl.BlockSpec((8, wide), lambda i: (jnp.maximum(i * (tm // 8) - 1, 0), 1)),
            pl.BlockSpec((8, wide), lambda i: (jnp.minimum((i + 1) * (tm // 8), nblk8 - 1), 1)),
            full(mu), full(vecs), full(dvec),
            full(g1p), full(g2p), full(w1p), full(w2p), full(a1p), full(a2p),
        ],
        out_specs=[row, row, row, row, row, row2, row2, row2],
        out_shape=[sds((m, GROUP_WIDTH), F32)] * 5 + [sds((2, m, GROUP_WIDTH), F32)] * 3,
        compiler_params=_cparams(("parallel",)),
        name="rwkv_prep",
    )(proj, proj, proj, mu, vecs, dvec, g1p, g2p, w1p, w2p, a1p, a2p)


RW_CHUNK = 64


def _bdot(a, b):
    return jnp.dot(a.astype(BF16), b.astype(BF16), preferred_element_type=F32)


def _bdot_nt(a, b):
    return lax.dot_general(a.astype(BF16), b.astype(BF16), (((1,), (1,)), ((), ())),
                           preferred_element_type=F32)


def _bdot_tn(a, b):
    return lax.dot_general(a.astype(BF16), b.astype(BF16), (((0,), (0,)), ((), ())),
                           preferred_element_type=F32)


def _rwkv_scan_kernel(rf, vf, nf, lf, kf, bf, rb, vb, nb, lb, kb, bb, yf_ref, yb_ref, st_ref,
                      *, bn):
    c = RW_CHUNK
    pw = 2 * RW_HEAD_DIM

    @pl.when(pl.program_id(0) == 0)
    def _():
        st_ref[...] = jnp.zeros_like(st_ref)

    row = lax.broadcasted_iota(jnp.int32, (c, pw), 0)
    col = lax.broadcasted_iota(jnp.int32, (c, pw), 1)
    first = col < RW_HEAD_DIM
    colh = col % RW_HEAD_DIM
    incl2 = {0: (colh <= row).astype(F32), 1: (colh >= row).astype(F32)}
    strict2 = {0: (colh < row).astype(F32), 1: (colh > row).astype(F32)}
    eye2 = (colh == row).astype(F32)
    trow = lax.broadcasted_iota(jnp.int32, (c, c), 0)
    tcol = lax.broadcasted_iota(jnp.int32, (c, c), 1)
    tri = {0: (tcol <= trow).astype(BF16), 1: (tcol >= trow).astype(BF16)}
    br = lax.broadcasted_iota(jnp.int32, (pw, pw), 0) // RW_HEAD_DIM
    bc = lax.broadcasted_iota(jnp.int32, (pw, pw), 1) // RW_HEAD_DIM
    same_head = br == bc

    def bd(x):
        return jnp.concatenate([jnp.where(first, x, 0.0), jnp.where(first, 0.0, x)], axis=0)

    npair = GROUP_WIDTH // pw
    cat = jnp.concatenate

    def body(b, carry):
        ch = []
        for d, refs in ((0, (rf, vf, nf, lf, kf, bf)), (1, (rb, vb, nb, lb, kb, bb))):
            r_, v_, a_, lw, k_, b_ = (ref[b] for ref in refs)
            hi = lw.astype(BF16)
            r1 = lw - hi.astype(F32)
            mid = r1.astype(BF16)
            lo = (r1 - mid.astype(F32)).astype(BF16)
            cs = (jnp.dot(tri[d], hi, preferred_element_type=F32)
                  + jnp.dot(tri[d], mid, preferred_element_type=F32)
                  + jnp.dot(tri[d], lo, preferred_element_type=F32))
            tot = cs[c - 1:c] if d == 0 else cs[0:1]
            g_incl = jnp.exp(cs)
            inv = jnp.exp(-cs)
            gend = jnp.exp(tot - cs)
            gtot = jnp.exp(tot)
            at = a_ * jnp.exp(cs - lw)
            bt = b_ * inv
            kt = k_ * inv
            rt = r_ * g_incl
            bh = b_ * gend
            kh = k_ * gend
            for p in range(npair):
                sl = slice(p * pw, (p + 1) * pw)
                ch.append(dict(d=d, idx=(b * 2 + d) * npair + p, vm=v_[:, sl], at=at[:, sl],
                               bt=bt[:, sl], kt=kt[:, sl], rt=rt[:, sl], bh=bh[:, sl],
                               kh=kh[:, sl], gtot=gtot[:, sl]))
        for q in ch:
            o = _bdot_nt(cat([q['at'], q['rt']], axis=0),
                         cat([bd(q['bt']), bd(q['kt'])], axis=0))
            q['a_ab'] = o[:c, :pw] * strict2[q['d']]
            q['a_ak'] = o[:c, pw:] * strict2[q['d']]
            q['q_bk'] = cat([o[c:, :pw] * incl2[q['d']], o[c:, pw:] * incl2[q['d']]], axis=1)
            q['tm'] = eye2 + q['a_ab']
        for q in ch:
            q['pp'] = _bdot(q['a_ab'], bd(q['a_ab']))
        for i in range(5):
            for q in ch:
                rr = _bdot(cat([q['tm'], q['pp']], axis=0), bd(q['pp']))
                q['tm'] = q['tm'] + rr[:c]
                q['pp'] = rr[c:]
        for q in ch:
            q['akv'] = _bdot(q['a_ak'], bd(q['vm']))
        for q in ch:
            q['w_uv'] = _bdot(q['tm'], cat([bd(q['at']), bd(q['akv'])], axis=1))
        for q in ch:
            q['s0'] = st_ref[q['idx']]
            q['uy'] = _bdot_nt(cat([q['w_uv'][:, :pw], q['rt']], axis=0), q['s0'])
        for q in ch:
            q['u'] = q['uy'][:c] + q['w_uv'][:, pw:]
            upd = _bdot_tn(cat([q['u'], q['vm']], axis=0), cat([q['bh'], q['kh']], axis=0))
            st_ref[q['idx']] = q['s0'] * q['gtot'] + jnp.where(same_head, upd, 0.0)
        for q in ch:
            q['y'] = q['uy'][c:] + _bdot(q['q_bk'], cat([bd(q['u']), bd(q['vm'])], axis=0))
        yf_ref[b] = cat([q['y'] for q in ch if q['d'] == 0], axis=1)
        yb_ref[b] = cat([q['y'] for q in ch if q['d'] == 1], axis=1)
        return carry

    lax.fori_loop(0, bn, body, 0)


def _rwkv_scan(r, v, nkk, w, kd, b):
    bn, seq, _ = r.shape
    nc = seq // RW_CHUNK
    blk = (bn, RW_CHUNK, GROUP_WIDTH)
    fwd = pl.BlockSpec(blk, lambda c: (0, c, 0))
    bwd = pl.BlockSpec(blk, lambda c: (0, nc - 1 - c, 0))
    fwd_d = pl.BlockSpec((None,) + blk, lambda c: (0, 0, c, 0))
    bwd_d = pl.BlockSpec((None,) + blk, lambda c: (1, 0, nc - 1 - c, 0))
    return pl.pallas_call(
        functools.partial(_rwkv_scan_kernel, bn=bn),
        grid=(nc,),
        in_specs=[fwd, fwd, fwd, fwd_d, fwd_d, fwd_d, bwd, bwd, bwd, bwd_d, bwd_d, bwd_d],
        out_specs=[fwd, bwd],
        out_shape=[jax.ShapeDtypeStruct((bn, seq, GROUP_WIDTH), F32)] * 2,
        scratch_shapes=[pltpu.VMEM((2 * bn * (RW_HEADS // 2), 2 * RW_HEAD_DIM, 2 * RW_HEAD_DIM), F32)],
        compiler_params=_cparams(("arbitrary",)),
        name="rwkv_scan",
    )(r, v, nkk, w, kd, b, r, v, nkk, w, kd, b)


def _rwkv_post_kernel(yf_ref, yb_ref, bonus_ref, g_ref, lnw_ref, lnb_ref, gain_ref, o_ref):
    ones = _head_ones(GROUP_WIDTH)
    y = yf_ref[...] + yb_ref[...]
    mean = _head_sum2(y, ones) * (1.0 / RW_HEAD_DIM)
    yc = y - mean
    var = _head_sum2(yc * yc, ones) * (1.0 / RW_HEAD_DIM)
    yn = yc * lax.rsqrt(var + RW_GN_EPS) * lnw_ref[...] + lnb_ref[...]
    out = (yn + bonus_ref[...]) * g_ref[...]
    o_ref[...] = _rms(out, gain_ref[...]).astype(BF16)


def _rwkv_post(yf, yb, bonus, g, ln_w, ln_b, gain, *, tm=512):
    m = yf.shape[0]
    row = pl.BlockSpec((tm, GROUP_WIDTH), lambda i: (i, 0))
    vec = pl.BlockSpec((1, GROUP_WIDTH), lambda i: (0, 0))
    v2 = lambda a: a.reshape(1, GROUP_WIDTH)
    return pl.pallas_call(
        _rwkv_post_kernel,
        grid=(m // tm,),
        in_specs=[row, row, row, row, vec, vec, vec],
        out_specs=row,
        out_shape=jax.ShapeDtypeStruct((m, GROUP_WIDTH), BF16),
        compiler_params=_cparams(("parallel",)),
        name="rwkv_post",
    )(yf, yb, bonus, g, v2(ln_w), v2(ln_b), v2(gain))


def _norm_rows_kernel(x_ref, g_ref, o_ref):
    o_ref[...] = _rms(x_ref[...], g_ref[...]).astype(BF16)


def _norm_rows(x, gain, *, tm=512):
    m, n = x.shape
    return pl.pallas_call(
        _norm_rows_kernel,
        grid=(m // tm,),
        in_specs=[pl.BlockSpec((tm, n), lambda i: (i, 0)), pl.BlockSpec((1, n), lambda i: (0, 0))],
        out_specs=pl.BlockSpec((tm, n), lambda i: (i, 0)),
        out_shape=jax.ShapeDtypeStruct((m, n), BF16),
        compiler_params=_cparams(("parallel",)),
        name="norm_rows",
    )(x, gain.reshape(1, n))


def _out_proj_kernel(ya_ref, yb_ref, yc_ref, yd_ref, w_ref, x_ref, o_ref):
    acc = x_ref[...]
    for g, y_ref in enumerate((ya_ref, yb_ref, yc_ref, yd_ref)):
        acc = acc + jnp.dot(y_ref[...], w_ref[g * GROUP_WIDTH:(g + 1) * GROUP_WIDTH, :],
                            preferred_element_type=F32)
    o_ref[...] = acc


def _out_proj(ya, yb, yc, yd, w_out, x, *, tm=512, tn=1024):
    m, n = x.shape
    row = pl.BlockSpec((tm, GROUP_WIDTH), lambda i, j: (i, 0))
    return pl.pallas_call(
        _out_proj_kernel,
        grid=(m // tm, n // tn),
        in_specs=[row, row, row, row,
                  pl.BlockSpec((N_MIXERS * GROUP_WIDTH, tn), lambda i, j: (0, j)),
                  pl.BlockSpec((tm, tn), lambda i, j: (i, j))],
        out_specs=pl.BlockSpec((tm, tn), lambda i, j: (i, j)),
        out_shape=jax.ShapeDtypeStruct((m, n), F32),
        compiler_params=_cparams(("parallel", "parallel")),
        name="out_proj",
    )(ya, yb, yc, yd, w_out.astype(BF16), x)


PEER_TOK = 256
NEG_INF = float("-inf")


def _peer_stats_kernel(q_ref, khi_ref, klo_ref, tau_o, e1_o, e2_o,
                       sc1_ref, sc2_ref, top_ref, cand_ref):
    nt = q_ref.shape[0]
    for h in range(PEER_HEADS):
        qh = q_ref[:, h * LANES:(h + 1) * LANES]
        qhi, qlo = _split2(qh)
        nt_dot = lambda a, b: lax.dot_general(a, b, (((1,), (1,)), ((), ())),
                                              preferred_element_type=F32)
        for p in range(2):
            sc = (nt_dot(khi_ref[h, p], qhi) + nt_dot(khi_ref[h, p], qlo)
                  + nt_dot(klo_ref[h, p], qhi))
            if p == 0:
                sc1_ref[...] = sc
            else:
                sc2_ref[...] = sc
            x = sc
            for i in range(PEER_TOPK):
                m = jnp.max(x, axis=0, keepdims=True)
                top_ref[p, i:i + 1, :] = m
                x = jnp.where(x == m, NEG_INF, x)
        s2 = top_ref[1]
        sub = lax.broadcasted_iota(jnp.int32, (8, nt), 0)
        cand_ref[0:16, :] = top_ref[0, 0:1, :] + s2
        cand_ref[16:24, :] = top_ref[0, 1:2, :] + s2[0:8]
        for g, (i, nj) in enumerate(((2, 5), (3, 4), (4, 3), (5, 2), (6, 2), (7, 2))):
            cand_ref[24 + 8 * g:32 + 8 * g, :] = jnp.where(sub < nj, top_ref[0, i:i + 1, :] + s2[0:8],
                                                          NEG_INF)
        cand_ref[72:80, :] = top_ref[0, 8:16, :] + s2[0:1]
        x = cand_ref[...]
        top = top_ref[0, 0:1, :] + top_ref[1, 0:1, :]
        zsum = jnp.zeros((1, nt), F32)
        m = top
        for i in range(PEER_TOPK):
            m = jnp.max(x, axis=0, keepdims=True)
            zsum = zsum + jnp.exp(m - top)
            x = jnp.where(x == m, NEG_INF, x)
        thr = m
        sc1 = sc1_ref[...]
        th = jnp.full(sc1.shape, jnp.inf, F32)
        for i in range(PEER_TOPK):
            s1i = top_ref[0, i:i + 1, :]
            row_th = jnp.min(jnp.where(s1i + s2 >= thr, s2, jnp.inf), axis=0, keepdims=True)
            th = jnp.where(sc1 == s1i, row_th, th)
        e1 = jnp.exp(sc1 - top_ref[0, 0:1, :]) / zsum
        e2 = jnp.exp(sc2_ref[...] - top_ref[1, 0:1, :])
        tau = jnp.exp(th - top_ref[1, 0:1, :])
        for t in range(nt // LANES):
            tl = slice(t * LANES, (t + 1) * LANES)
            tau_o[h, t] = tau[:, tl]
            e1_o[h, t] = e1[:, tl]
            e2_o[h, t] = e2[:, tl]


def _peer_stats(q, sub_keys):
    ntok = q.shape[0]
    keys = jnp.zeros((PEER_HEADS, 2, PEER_N_KEYS, LANES), F32)
    keys = keys.at[:, 0, :, :PEER_HALF].set(sub_keys[:, 0]).at[:, 1, :, PEER_HALF:].set(sub_keys[:, 1])
    khi = keys.astype(BF16)
    klo = (keys - khi.astype(F32)).astype(BF16)
    kspec = pl.BlockSpec(keys.shape, lambda i: (0, 0, 0, 0))
    big = pl.BlockSpec((PEER_HEADS, PEER_TOK // LANES, PEER_N_KEYS, LANES), lambda i: (0, i, 0, 0))
    return pl.pallas_call(
        _peer_stats_kernel,
        grid=(ntok // PEER_TOK,),
        in_specs=[pl.BlockSpec((PEER_TOK, PEER_HEADS * LANES), lambda i: (i, 0)), kspec, kspec],
        out_specs=[big] * 3,
        out_shape=[jax.ShapeDtypeStruct((PEER_HEADS, ntok // LANES, PEER_N_KEYS, LANES), F32)] * 3,
        scratch_shapes=[pltpu.VMEM((PEER_N_KEYS, PEER_TOK), F32),
                        pltpu.VMEM((PEER_N_KEYS, PEER_TOK), F32),
                        pltpu.VMEM((2, PEER_TOPK, PEER_TOK), F32),
                        pltpu.VMEM((80, PEER_TOK), F32)],
        compiler_params=_cparams(("parallel",)),
        name="peer_stats",
    )(q, khi, klo)


PEER_TB = 512
PEER_ET = 512


PEER_TC = 256


def _peer_act_kernel(hn_ref, u_ref, g_ref):
    act = lax.dot_general(u_ref[...], hn_ref[...], (((1,), (1,)), ((), ())),
                          preferred_element_type=F32)
    g_ref[...] = (0.5 * act * (1.0 + lax.erf(act * (1.0 / math.sqrt(2.0))))).astype(BF16)


def _peer_act(hn, u):
    ntok, d = hn.shape
    nexp = u.shape[0]
    return pl.pallas_call(
        _peer_act_kernel,
        grid=(ntok // PEER_TB, nexp // PEER_ET),
        in_specs=[pl.BlockSpec((PEER_TB, d), lambda i, j: (i, 0)),
                  pl.BlockSpec((PEER_ET, d), lambda i, j: (j, 0))],
        out_specs=pl.BlockSpec((PEER_ET, PEER_TB), lambda i, j: (j, i)),
        out_shape=jax.ShapeDtypeStruct((nexp, ntok), BF16),
        compiler_params=_cparams(("parallel", "parallel")),
        name="peer_act",
    )(hn, u)


def _peer_mix_kernel(vt_ref, g_ref, tau_ref, e1_ref, e2_ref, x_ref, o_ref, acc_ref, w_ref):
    j = pl.program_id(1)
    a0 = j * (PEER_ET // PEER_N_KEYS)

    @pl.when(j == 0)
    def _():
        acc_ref[...] = jnp.zeros_like(acc_ref)

    for c in range(PEER_TB // PEER_TC):
        cl = slice(c * PEER_TC, (c + 1) * PEER_TC)
        for t in range(c * PEER_TC // LANES, (c + 1) * PEER_TC // LANES):
            tl = slice(t * LANES, (t + 1) * LANES)
            for ai in range(PEER_ET // PEER_N_KEYS):
                gate = jnp.zeros((PEER_N_KEYS, LANES), F32)
                for h in range(PEER_HEADS):
                    tau = tau_ref[h, t, pl.ds(a0 + ai, 1), :]
                    e1 = e1_ref[h, t, pl.ds(a0 + ai, 1), :]
                    e2 = e2_ref[h, t]
                    gate = gate + jnp.where(e2 >= tau, e2, 0.0) * e1
                rows = slice(ai * PEER_N_KEYS, (ai + 1) * PEER_N_KEYS)
                w_ref[rows, tl] = (gate * g_ref[rows, tl].astype(F32)).astype(BF16)
        acc_ref[:, cl] += jnp.dot(vt_ref[...], w_ref[:, cl], preferred_element_type=F32)

    @pl.when(j == pl.num_programs(1) - 1)
    def _():
        o_ref[...] = x_ref[...] + acc_ref[...].T


def _peer_mix(vt, g, tau, e1, e2, x):
    ntok, d = x.shape
    nexp = vt.shape[1]
    stat = pl.BlockSpec((PEER_HEADS, PEER_TB // LANES, PEER_N_KEYS, LANES), lambda i, j: (0, i, 0, 0))
    tok = pl.BlockSpec((PEER_TB, d), lambda i, j: (i, 0))
    return pl.pallas_call(
        _peer_mix_kernel,
        grid=(ntok // PEER_TB, nexp // PEER_ET),
        in_specs=[pl.BlockSpec((d, PEER_ET), lambda i, j: (0, j)),
                  pl.BlockSpec((PEER_ET, PEER_TB), lambda i, j: (j, i)),
                  stat, stat, stat, tok],
        out_specs=tok,
        out_shape=jax.ShapeDtypeStruct((ntok, d), F32),
        scratch_shapes=[pltpu.VMEM((d, PEER_TB), F32), pltpu.VMEM((PEER_ET, PEER_TB), BF16)],
        compiler_params=_cparams(("parallel", "arbitrary")),
        name="peer_mix",
    )(vt, g, tau, e1, e2, x)


def _peer_ffn(x, norm_gain, w_query, sub_keys, expert_u, expert_v):
    q, hn = _norm_matmul(x, norm_gain, w_query.astype(BF16))
    tau, e1, e2 = _peer_stats(q, sub_keys)
    g = _peer_act(hn, expert_u.astype(BF16))
    return _peer_mix(expert_v.astype(BF16).T, g, tau, e1, e2, x)


def kernel(x, norm_mix, w_in, fourier_w, s5_lam_re, s5_lam_im, s5_log_step, s5_b_re, s5_b_im, s5_c_re, s5_c_im, s5_d, s5_w_glu, attn_q_norm, attn_k_norm, rwkv_mu, rwkv_w0, rwkv_w1, rwkv_w2, rwkv_a0, rwkv_a1, rwkv_a2, rwkv_g1, rwkv_g2, rwkv_k_k, rwkv_k_a, rwkv_r_k, rwkv_ln_w, rwkv_ln_b, branch_norm, w_out, norm_ffn, peer_w_query, peer_sub_keys, peer_u, peer_v):
    bn, seq, d = x.shape
    m = bn * seq
    depth = w_in.shape[0]
    dft_s, dft_c = _dft_tables(seq)
    cc, ss = _rope_tables(seq)
    xf = x.reshape(m, d)
    for l in range(depth):
        bgain = branch_norm[l].reshape(N_MIXERS, GROUP_WIDTH)
        proj, _ = _norm_matmul(xf, norm_mix[l], w_in[l].astype(BF16))
        proj3 = proj.reshape(bn, seq, PROJ_WIDTH)

        ya = _fnet_mix(proj3, dft_s, dft_c, fourier_w[l], bgain[0]).reshape(m, GROUP_WIDTH)

        b_blk, c_blk, lam1, pw = _s5_params(s5_lam_re[l], s5_lam_im[l], s5_log_step[l],
                                            s5_b_re[l], s5_b_im[l], s5_c_re[l], s5_c_im[l])
        u_s5 = _s5_row_order(proj3[:, :, GROUP_WIDTH:2 * GROUP_WIDTH])
        yf, yb = _s5_scan(u_s5, b_blk, c_blk, lam1, pw)
        ybm = _s5_tail(yf.reshape(m, GROUP_WIDTH), yb.reshape(m, GROUP_WIDTH),
                       u_s5.reshape(m, GROUP_WIDTH), s5_d[l], s5_w_glu[l], bgain[1])
        ybm = _s5_row_order(ybm.reshape(bn, seq, GROUP_WIDTH), inverse=True).reshape(m, GROUP_WIDTH)

        att = _attention_mix(proj3, attn_q_norm[l], attn_k_norm[l], cc, ss)
        yc = _norm_rows(att.reshape(m, GROUP_WIDTH), bgain[2])

        r, v, nkk, g, bonus, w, kd, b = _rwkv_prep(
            proj, seq, rwkv_mu[l], rwkv_w0[l], rwkv_w1[l], rwkv_w2[l], rwkv_a0[l], rwkv_a1[l],
            rwkv_a2[l], rwkv_g1[l], rwkv_g2[l], rwkv_k_k[l], rwkv_k_a[l], rwkv_r_k[l])
        s3 = lambda a: a.reshape(bn, seq, GROUP_WIDTH)
        s4 = lambda a: a.reshape(2, bn, seq, GROUP_WIDTH)
        y_f, y_b = _rwkv_scan(s3(r), s3(v), s3(nkk), s4(w), s4(kd), s4(b))
        yd = _rwkv_post(y_f.reshape(m, GROUP_WIDTH), y_b.reshape(m, GROUP_WIDTH), bonus, g,
                        rwkv_ln_w[l], rwkv_ln_b[l], bgain[3])

        xf = _out_proj(ya, ybm, yc, yd, w_out[l], xf)
        xf = _peer_ffn(xf, norm_ffn[l], peer_w_query[l], peer_sub_keys[l], peer_u[l], peer_v[l])
    return xf.reshape(bn, seq, d)
```

```python
import functools
import math

import jax
import jax.numpy as jnp
import numpy as np
from jax import lax
from jax.experimental import pallas as pl
from jax.experimental.pallas import tpu as pltpu

F32 = jnp.float32
BF16 = jnp.bfloat16

D_MODEL = 2048
GROUP_WIDTH = 512
N_MIXERS = 4
FN_HEADS = 4
FN_HEAD_DIM = 128
S5_GROUP_CH = 16
S5_GROUPS = 32
S5_STATE = 64
S5_NSTATE = S5_GROUPS * S5_STATE
HEAD_DIM = 128
ATT_HEADS = 4
KV_HEADS = 2
GRID_W = 64
ROPE_THETA = 10000.0
AXIS_ROT_DIM = 64
RW_HEAD_DIM = 64
RW_HEADS = 8
RW_GN_EPS = 64e-5
LORA_PAD = 128
PEER_HEADS = 8
PEER_HALF = 64
PEER_N_KEYS = 128
PEER_TOPK = 16
NORM_EPS = 1e-6
PROJ_WIDTH = 4096

LANES = 128
VMEM_LIMIT = 56 * 1024 * 1024


def _cparams(sem):
    return pltpu.CompilerParams(dimension_semantics=sem, vmem_limit_bytes=VMEM_LIMIT)


def _rms(x, gain):
    return x * lax.rsqrt(jnp.mean(x * x, axis=-1, keepdims=True) + NORM_EPS) * gain


def _split2(x):
    hi = x.astype(BF16)
    lo = (x - hi.astype(F32)).astype(BF16)
    return hi, lo


def _norm_matmul_kernel(x_ref, g_ref, w_ref, o_ref, xn_out_ref, xn_ref):
    @pl.when(pl.program_id(1) == 0)
    def _():
        xn = _rms(x_ref[...], g_ref[...]).astype(BF16)
        xn_ref[...] = xn
        xn_out_ref[...] = xn

    o_ref[...] = jnp.dot(xn_ref[...], w_ref[...], preferred_element_type=F32)


def _norm_matmul(x, gain, w, *, tm=512, tn=1024):
    m, k = x.shape
    n = w.shape[1]
    return pl.pallas_call(
        _norm_matmul_kernel,
        grid=(m // tm, n // tn),
        in_specs=[
            pl.BlockSpec((tm, k), lambda i, j: (i, 0)),
            pl.BlockSpec((1, k), lambda i, j: (0, 0)),
            pl.BlockSpec((k, tn), lambda i, j: (0, j)),
        ],
        out_specs=[
            pl.BlockSpec((tm, tn), lambda i, j: (i, j)),
            pl.BlockSpec((tm, k), lambda i, j: (i, 0)),
        ],
        out_shape=[
            jax.ShapeDtypeStruct((m, n), F32),
            jax.ShapeDtypeStruct((m, k), BF16),
        ],
        scratch_shapes=[pltpu.VMEM((tm, k), BF16)],
        compiler_params=_cparams(("parallel", "arbitrary")),
        name="norm_matmul",
    )(x, gain.reshape(1, k), w)


def _fnet_kernel(pa_ref, dftc_ref, dfts_ref, wf_ref, g_ref, o_ref, z_ref, *, seq):
    @pl.when(pl.program_id(1) == 0)
    def _():
        for h in range(FN_HEADS):
            sl = slice(h * FN_HEAD_DIM, (h + 1) * FN_HEAD_DIM)
            xh = pa_ref[0, :, sl].astype(BF16)
            t = jnp.dot(xh, dftc_ref[...], preferred_element_type=F32)
            z_ref[0:seq, sl] = t[:, :FN_HEAD_DIM].astype(BF16)
            z_ref[seq:2 * seq, sl] = t[:, FN_HEAD_DIM:].astype(BF16)

    scale = 1.0 / math.sqrt(seq * FN_HEAD_DIM)
    re = jnp.dot(dfts_ref[...], z_ref[...], preferred_element_type=F32) * scale
    ya = jnp.dot(re.astype(BF16), wf_ref[...], preferred_element_type=F32)
    o_ref[0] = _rms(ya, g_ref[...]).astype(BF16)


def _dft_tables(seq):
    def cos_sin(n):
        k = np.arange(n, dtype=np.int64)
        ang = ((k[:, None] * k[None, :]) % n).astype(np.float64) * (2.0 * math.pi / n)
        return np.cos(ang), np.sin(ang)

    cs, sn = cos_sin(seq)
    dft_s = jnp.asarray(np.concatenate([cs, -sn], axis=1).astype(np.float32)).astype(BF16)
    cs, sn = cos_sin(FN_HEAD_DIM)
    dft_c = jnp.asarray(np.concatenate([cs, sn], axis=1).astype(np.float32)).astype(BF16)
    return dft_s, dft_c


def _fnet_mix(proj3, dft_s, dft_c, w_f, gain, *, tm=512):
    bn, seq, _ = proj3.shape
    return pl.pallas_call(
        functools.partial(_fnet_kernel, seq=seq),
        grid=(bn, seq // tm),
        in_specs=[
            pl.BlockSpec((1, seq, GROUP_WIDTH), lambda b, r: (b, 0, 0)),
            pl.BlockSpec((FN_HEAD_DIM, 2 * FN_HEAD_DIM), lambda b, r: (0, 0)),
            pl.BlockSpec((tm, 2 * seq), lambda b, r: (r, 0)),
            pl.BlockSpec((GROUP_WIDTH, GROUP_WIDTH), lambda b, r: (0, 0)),
            pl.BlockSpec((1, GROUP_WIDTH), lambda b, r: (0, 0)),
        ],
        out_specs=pl.BlockSpec((1, tm, GROUP_WIDTH), lambda b, r: (b, r, 0)),
        out_shape=jax.ShapeDtypeStruct((bn, seq, GROUP_WIDTH), BF16),
        scratch_shapes=[pltpu.VMEM((2 * seq, GROUP_WIDTH), BF16)],
        compiler_params=_cparams(("parallel", "arbitrary")),
        name="fnet_mix",
    )(proj3, dft_c, dft_s, w_f.astype(BF16), gain.reshape(1, GROUP_WIDTH))


S5_CHUNK = 256
S5_SEGS = 8
S5_STEPS = S5_CHUNK // S5_SEGS
S5_TILES_PER_PASS = 4
S5_HALF_CH = GROUP_WIDTH // 2


def _s5_params(lam_re, lam_im, log_step, b_re, b_im, c_re, c_im):
    step = jnp.exp(log_step.astype(F32))[..., None]
    lr = lam_re.astype(F32)
    li = lam_im.astype(F32)
    ar = lr * step
    ai = li * step
    mag = jnp.exp(ar)
    lbr = mag * jnp.cos(ai)
    lbi = mag * jnp.sin(ai)
    den = lr * lr + li * li
    qr = ((lbr - 1.0) * lr + lbi * li) / den
    qi = (lbi * lr - (lbr - 1.0) * li) / den
    bbr = qr[..., None] * b_re - qi[..., None] * b_im
    bbi = qr[..., None] * b_im + qi[..., None] * b_re
    eye = jnp.eye(S5_GROUPS, dtype=F32)

    def in_blk(b):
        t = jnp.einsum('dgph,gk->dghkp', b, eye)
        return t.reshape(2, GROUP_WIDTH, S5_NSTATE)

    def out_blk(c):
        t = jnp.einsum('dghp,gk->dgpkh', c, eye)
        return t.reshape(2, S5_NSTATE, GROUP_WIDTH)

    hc, hs = S5_HALF_CH, S5_NSTATE // 2
    halves = range(2)
    bre, bim = in_blk(bbr), in_blk(bbi)
    b_blk = jnp.stack([jnp.concatenate([m[:, r * hc:(r + 1) * hc, r * hs:(r + 1) * hs]
                                        for m in (bre, bim)], axis=2) for r in halves],
                      axis=1).astype(BF16)
    cre, cim = out_blk(c_re.astype(F32)), -out_blk(c_im.astype(F32))
    c_blk = jnp.stack([jnp.concatenate([m[:, r * hs:(r + 1) * hs, r * hc:(r + 1) * hc]
                                        for m in (cre, cim)], axis=1) for r in halves],
                      axis=1).astype(BF16)
    kpow = jnp.arange(1, S5_STEPS + 1, dtype=F32)[None, :, None, None]
    pmag = jnp.exp(ar[:, None] * kpow)
    pang = ai[:, None] * kpow
    pw = jnp.concatenate([(pmag * jnp.cos(pang)).reshape(2, S5_STEPS, S5_NSTATE),
                          (pmag * jnp.sin(pang)).reshape(2, S5_STEPS, S5_NSTATE)], axis=2)
    lam1 = pw[:, 0:1, :]
    pw_tiles = pw.reshape(2, S5_STEPS, 2 * S5_NSTATE // LANES, LANES).transpose(0, 2, 1, 3)
    return b_blk, c_blk, lam1, pw_tiles


def _s5_scan_kernel(uf_ref, ub_ref, bblk_ref, cblk_ref, lam_ref, pw_ref, yf_ref, yb_ref,
                    bu_ref, carry_ref, cin_ref):
    ns = S5_NSTATE

    @pl.when(pl.program_id(1) == 0)
    def _():
        carry_ref[...] = jnp.zeros_like(carry_ref)

    nt = ns // LANES
    th = nt // 2
    for d, u_ref in ((0, uf_ref), (1, ub_ref)):
        u = u_ref[0].astype(BF16)
        for rb in range(2):
            bu = jnp.dot(u[:, rb * S5_HALF_CH:(rb + 1) * S5_HALF_CH], bblk_ref[d, rb],
                         preferred_element_type=F32)
            for k in range(th):
                bu_ref[d, rb * th + k] = bu[:, k * LANES:(k + 1) * LANES]
                bu_ref[d, nt + rb * th + k] = bu[:, (th + k) * LANES:(th + k + 1) * LANES]

    def lanes(q):
        return slice(q * LANES, (q + 1) * LANES)

    for q0 in range(0, nt, S5_TILES_PER_PASS):
        tiles = range(q0, q0 + S5_TILES_PER_PASS)
        lam = {(d, q): (lam_ref[d, :, lanes(q)], lam_ref[d, :, lanes(nt + q)])
               for d in range(2) for q in tiles}

        def step(i, hs, tiles=tiles, lam=lam):
            out = []
            for d in range(2):
                row = i if d == 0 else S5_STEPS - 1 - i
                idx = pl.ds(pl.multiple_of(row * S5_SEGS, S5_SEGS), S5_SEGS)
                for q in tiles:
                    hr, hi = hs[len(out)], hs[len(out) + 1]
                    lr, li = lam[d, q]
                    nr = lr * hr - li * hi + bu_ref[d, q, idx, :]
                    ni = lr * hi + li * hr + bu_ref[d, nt + q, idx, :]
                    bu_ref[d, q, idx, :] = nr
                    bu_ref[d, nt + q, idx, :] = ni
                    out += [nr, ni]
            return tuple(out)

        z = jnp.zeros((S5_SEGS, LANES), F32)
        ends = lax.fori_loop(0, S5_STEPS, step, (z,) * (4 * S5_TILES_PER_PASS))

        cin = {}
        pos = 0
        for d in range(2):
            order = range(S5_SEGS) if d == 0 else range(S5_SEGS - 1, -1, -1)
            for q in tiles:
                er, ei = ends[pos], ends[pos + 1]
                pos += 2
                pr = pw_ref[d, q, S5_STEPS - 1:S5_STEPS, :]
                pi = pw_ref[d, nt + q, S5_STEPS - 1:S5_STEPS, :]
                cr = carry_ref[d, :, lanes(q)]
                ci = carry_ref[d, :, lanes(nt + q)]
                for sg in order:
                    cin_ref[d, sg:sg + 1, lanes(q)] = cr
                    cin_ref[d, sg:sg + 1, lanes(nt + q)] = ci
                    nr = er[sg:sg + 1] + pr * cr - pi * ci
                    ni = ei[sg:sg + 1] + pr * ci + pi * cr
                    cr, ci = nr, ni
                carry_ref[d, :, lanes(q)] = cr
                carry_ref[d, :, lanes(nt + q)] = ci
                cin[d, q] = (cin_ref[d, :, lanes(q)], cin_ref[d, :, lanes(nt + q)])

        def fix(i, c, tiles=tiles, cin=cin):
            for d in range(2):
                row = i if d == 0 else S5_STEPS - 1 - i
                idx = pl.ds(pl.multiple_of(row * S5_SEGS, S5_SEGS), S5_SEGS)
                for q in tiles:
                    pr = pw_ref[d, q, pl.ds(i, 1), :]
                    pi = pw_ref[d, nt + q, pl.ds(i, 1), :]
                    cr, ci = cin[d, q]
                    bu_ref[d, q, idx, :] = bu_ref[d, q, idx, :] + pr * cr - pi * ci
                    bu_ref[d, nt + q, idx, :] = bu_ref[d, nt + q, idx, :] + pr * ci + pi * cr
            return c

        lax.fori_loop(0, S5_STEPS, fix, 0)

    for d, y_ref in ((0, yf_ref), (1, yb_ref)):
        ys = []
        for cb in range(2):
            tiles = [cb * th + k for k in range(th)] + [nt + cb * th + k for k in range(th)]
            hs = jnp.concatenate([bu_ref[d, q].astype(BF16) for q in tiles], axis=1)
            ys.append(jnp.dot(hs, cblk_ref[d, cb], preferred_element_type=F32))
        y_ref[0] = jnp.concatenate(ys, axis=1)


def _s5_row_order(x, inverse=False):
    bn, seq, n = x.shape
    a, b = (S5_STEPS, S5_SEGS) if inverse else (S5_SEGS, S5_STEPS)
    return x.reshape(bn, seq // S5_CHUNK, a, b, n).transpose(0, 1, 3, 2, 4).reshape(bn, seq, n)


def _s5_scan(u, b_blk, c_blk, lam1, pw):
    bn, seq, _ = u.shape
    nc = seq // S5_CHUNK
    ns2 = 2 * S5_NSTATE
    u_blk = (1, S5_CHUNK, GROUP_WIDTH)
    return pl.pallas_call(
        _s5_scan_kernel,
        grid=(bn, nc),
        in_specs=[
            pl.BlockSpec(u_blk, lambda b, c: (b, c, 0)),
            pl.BlockSpec(u_blk, lambda b, c: (b, nc - 1 - c, 0)),
            pl.BlockSpec((2, 2, S5_HALF_CH, S5_NSTATE), lambda b, c: (0, 0, 0, 0)),
            pl.BlockSpec((2, 2, S5_NSTATE, S5_HALF_CH), lambda b, c: (0, 0, 0, 0)),
            pl.BlockSpec((2, 1, ns2), lambda b, c: (0, 0, 0)),
            pl.BlockSpec((2, ns2 // LANES, S5_STEPS, LANES), lambda b, c: (0, 0, 0, 0)),
        ],
        out_specs=[
            pl.BlockSpec(u_blk, lambda b, c: (b, c, 0)),
            pl.BlockSpec(u_blk, lambda b, c: (b, nc - 1 - c, 0)),
        ],
        out_shape=[jax.ShapeDtypeStruct((bn, seq, GROUP_WIDTH), F32)] * 2,
        scratch_shapes=[
            pltpu.VMEM((2, ns2 // LANES, S5_CHUNK, LANES), F32),
            pltpu.VMEM((2, 1, ns2), F32),
            pltpu.VMEM((2, S5_SEGS, ns2), F32),
        ],
        compiler_params=_cparams(("parallel", "arbitrary")),
        name="s5_scan",
    )(u, u, b_blk, c_blk, lam1, pw)


def _s5_tail_kernel(yf_ref, yb_ref, u_ref, d_ref, wg_ref, g_ref, o_ref):
    y = yf_ref[...] + yb_ref[...] + d_ref[...] * u_ref[...]
    y = jax.nn.gelu(y)
    gate = jnp.dot(y.astype(BF16), wg_ref[...], preferred_element_type=F32)
    o_ref[...] = _rms(y * jax.nn.sigmoid(gate), g_ref[...]).astype(BF16)


def _s5_tail(yf, yb, u, d_skip, w_glu, gain, *, tm=512):
    m = yf.shape[0]
    row = pl.BlockSpec((tm, GROUP_WIDTH), lambda i: (i, 0))
    vec = pl.BlockSpec((1, GROUP_WIDTH), lambda i: (0, 0))
    return pl.pallas_call(
        _s5_tail_kernel,
        grid=(m // tm,),
        in_specs=[row, row, row, vec,
                  pl.BlockSpec((GROUP_WIDTH, GROUP_WIDTH), lambda i: (0, 0)), vec],
        out_specs=row,
        out_shape=jax.ShapeDtypeStruct((m, GROUP_WIDTH), BF16),
        compiler_params=_cparams(("parallel",)),
        name="s5_tail",
    )(yf, yb, u, d_skip.reshape(1, GROUP_WIDTH), w_glu.astype(BF16), gain.reshape(1, GROUP_WIDTH))


def _rope(x, cc, ss):
    return x * cc + pltpu.roll(x, HEAD_DIM // 2, axis=1) * ss


def _attn_kernel(q_ref, k_ref, v_ref, qg_ref, kg_ref, ccq_ref, ssq_ref, cck_ref, ssk_ref,
                 o_ref, kp_ref, vp_ref):
    @pl.when(pl.program_id(2) == 0)
    def _():
        kn = _rms(k_ref[0], kg_ref[...])
        kp_ref[...] = _rope(kn, cck_ref[...], ssk_ref[...]).astype(BF16)
        vp_ref[...] = v_ref[0].astype(BF16)

    rep = ATT_HEADS // KV_HEADS
    outs = []
    for r in range(rep):
        q = q_ref[0, :, r * HEAD_DIM:(r + 1) * HEAD_DIM]
        qn = _rope(_rms(q, qg_ref[...]), ccq_ref[...], ssq_ref[...]) * (HEAD_DIM ** -0.5)
        s = lax.dot_general(qn.astype(BF16), kp_ref[...], (((1,), (1,)), ((), ())),
                            preferred_element_type=F32)
        p = jnp.exp(s - jnp.max(s, axis=-1, keepdims=True))
        inv = 1.0 / jnp.sum(p, axis=-1, keepdims=True)
        o = jnp.dot(p.astype(BF16), vp_ref[...], preferred_element_type=F32) * inv
        outs.append(o)
    o_ref[0] = jnp.concatenate(outs, axis=1)


def _rope_tables(seq):
    rows = seq // GRID_W
    row_id = jnp.repeat(jnp.arange(rows), GRID_W).astype(F32)
    col_id = jnp.tile(jnp.arange(GRID_W), rows).astype(F32)
    inv = ROPE_THETA ** (-jnp.arange(0, AXIS_ROT_DIM, 2, dtype=F32) / AXIS_ROT_DIM)
    ang = jnp.concatenate([row_id[:, None] * inv, col_id[:, None] * inv], axis=-1)
    cos, sin = jnp.cos(ang), jnp.sin(ang)
    return jnp.concatenate([cos, cos], axis=1), jnp.concatenate([-sin, sin], axis=1)


def _attention_mix(proj3, q_gain, k_gain, cc, ss, *, tq=256):
    bn, seq, _ = proj3.shape
    rep = ATT_HEADS // KV_HEADS
    qw = rep * HEAD_DIM
    q_col0 = (2 * GROUP_WIDTH) // qw
    k_col0 = (3 * GROUP_WIDTH) // HEAD_DIM
    v_col0 = k_col0 + KV_HEADS
    vec = pl.BlockSpec((1, HEAD_DIM), lambda b, g, i: (0, 0))
    return pl.pallas_call(
        _attn_kernel,
        grid=(bn, KV_HEADS, seq // tq),
        in_specs=[
            pl.BlockSpec((1, tq, qw), lambda b, g, i: (b, i, q_col0 + g)),
            pl.BlockSpec((1, seq, HEAD_DIM), lambda b, g, i: (b, 0, k_col0 + g)),
            pl.BlockSpec((1, seq, HEAD_DIM), lambda b, g, i: (b, 0, v_col0 + g)),
            vec, vec,
            pl.BlockSpec((tq, HEAD_DIM), lambda b, g, i: (i, 0)),
            pl.BlockSpec((tq, HEAD_DIM), lambda b, g, i: (i, 0)),
            pl.BlockSpec((seq, HEAD_DIM), lambda b, g, i: (0, 0)),
            pl.BlockSpec((seq, HEAD_DIM), lambda b, g, i: (0, 0)),
        ],
        out_specs=pl.BlockSpec((1, tq, qw), lambda b, g, i: (b, i, g)),
        out_shape=jax.ShapeDtypeStruct((bn, seq, GROUP_WIDTH), F32),
        scratch_shapes=[pltpu.VMEM((seq, HEAD_DIM), BF16), pltpu.VMEM((seq, HEAD_DIM), BF16)],
        compiler_params=_cparams(("parallel", "parallel", "arbitrary")),
        name="gqa_attention",
    )(proj3, proj3, proj3, q_gain.reshape(1, HEAD_DIM), k_gain.reshape(1, HEAD_DIM),
      cc, ss, cc, ss)


def _head_ones(width):
    r = lax.broadcasted_iota(jnp.int32, (width, width), 0) // RW_HEAD_DIM
    c = lax.broadcasted_iota(jnp.int32, (width, width), 1) // RW_HEAD_DIM
    return (r == c).astype(BF16)


def _head_sum2(x, ones):
    hi, lo = _split2(x)
    return (jnp.dot(hi, ones, preferred_element_type=F32)
            + jnp.dot(lo, ones, preferred_element_type=F32))


def _rwkv_prep_kernel(c_ref, p_ref, n_ref, mu_ref, vecs_ref, dvec_ref,
                      g1_ref, g2_ref, w1_ref, w2_ref, a1_ref, a2_ref,
                      r_o, v_o, nkk_o, g_o, bonus_o, w_o, kd_o, b_o, *, tiles_per_seq):
    i = pl.program_id(0)
    tm = c_ref.shape[0]
    first = (i % tiles_per_seq) == 0
    last = (i % tiles_per_seq) == tiles_per_seq - 1
    rows = lax.broadcasted_iota(jnp.int32, (tm, GROUP_WIDTH), 0)
    ones = _head_ones(GROUP_WIDTH)

    def shifted(j):
        sl = slice(j * GROUP_WIDTH, (j + 1) * GROUP_WIDTH)
        x = c_ref[:, sl]
        prev_row = jnp.where(first, 0.0, p_ref[7:8, sl])
        next_row = jnp.where(last, 0.0, n_ref[0:1, sl])
        prev = jnp.where(rows == 0, prev_row, pltpu.roll(x, 1, axis=0))
        nxt = jnp.where(rows == tm - 1, next_row, pltpu.roll(x, tm - 1, axis=0))
        return x + (prev - x) * mu_ref[j, 0:1, :] + (nxt - x) * mu_ref[j, 1:2, :]

    r, k, v, z = (shifted(j) for j in range(4))
    k_k, k_a, r_k = vecs_ref[0:1, :], vecs_ref[1:2, :], vecs_ref[2:3, :]
    zb = z.astype(BF16)

    gmid = jax.nn.sigmoid(jnp.dot(zb, g1_ref[...], preferred_element_type=F32))
    g_o[...] = jnp.dot(gmid.astype(BF16), g2_ref[...], preferred_element_type=F32)

    kk = k * k_k
    kk = kk * lax.rsqrt(_head_sum2(kk * kk, ones) + 1e-12)
    r_o[...] = r
    v_o[...] = v
    nkk_o[...] = -kk

    bonus = jnp.zeros_like(r)
    for d in range(2):
        w0, a0 = dvec_ref[d, 0:1, :], dvec_ref[d, 1:2, :]
        wmid = jnp.tanh(jnp.dot(zb, w1_ref[d], preferred_element_type=F32))
        wlin = w0 + jnp.dot(wmid.astype(BF16), w2_ref[d], preferred_element_type=F32)
        w_log = -jax.nn.softplus(-wlin) - 0.5
        w_o[d] = -jnp.exp(w_log)
        amid = jnp.dot(zb, a1_ref[d], preferred_element_type=F32)
        a = jax.nn.sigmoid(a0 + jnp.dot(amid.astype(BF16), a2_ref[d], preferred_element_type=F32))
        kd = k * (1.0 + (a - 1.0) * k_a)
        kd_o[d] = kd
        b_o[d] = kk * a
        bonus = bonus + _head_sum2(r * kd * r_k, ones) * v
    bonus_o[...] = bonus


def _pad_cols(w, n):
    return jnp.pad(w, [(0, 0)] * (w.ndim - 1) + [(0, n - w.shape[-1])])


def _pad_rows(w, n):
    return jnp.pad(w, [(0, 0)] * (w.ndim - 2) + [(0, n - w.shape[-2]), (0, 0)])


def _rwkv_prep(proj, seq, mu, w0, w1, w2, a0, a1, a2, g1, g2, k_k, k_a, r_k, *, tm=512):
    m = proj.shape[0]
    wide = 4 * GROUP_WIDTH
    tiles_per_seq = seq // tm
    nblk8 = m // 8
    vecs = jnp.stack([k_k, k_a, r_k.reshape(GROUP_WIDTH)]
                     + [jnp.zeros((GROUP_WIDTH,), F32)] * 5).astype(F32)
    dvec = jnp.stack([jnp.stack([w0[d], a0[d]] + [jnp.zeros((GROUP_WIDTH,), F32)] * 6)
                      for d in range(2)]).astype(F32)
    row = pl.BlockSpec((tm, GROUP_WIDTH), lambda i: (i, 0))
    row2 = pl.BlockSpec((2, tm, GROUP_WIDTH), lambda i: (0, i, 0))
    full = lambda a: pl.BlockSpec(a.shape, lambda i: (0,) * a.ndim)
    g1p = _pad_cols(g1, LORA_PAD).astype(BF16)
    g2p = _pad_rows(g2, LORA_PAD).astype(BF16)
    w1p = _pad_cols(w1, LORA_PAD).astype(BF16)
    w2p = _pad_rows(w2, LORA_PAD).astype(BF16)
    a1p = _pad_cols(a1, LORA_PAD).astype(BF16)
    a2p = _pad_rows(a2, LORA_PAD).astype(BF16)
    sds = jax.ShapeDtypeStruct
    return pl.pallas_call(
        functools.partial(_rwkv_prep_kernel, tiles_per_seq=tiles_per_seq),
        grid=(m // tm,),
        in_specs=[
            pl.BlockSpec((tm, wide), lambda i: (i, 1)),
            pl.BlockSpec((8, wide), lambda i: (jnp.maximum(i * (tm // 8) - 1, 0), 1)),
            pl.BlockSpec((8, wide), lambda i: (jnp.minimum((i + 1) * (tm // 8), nblk8 - 1), 1)),
            full(mu), full(vecs), full(dvec),
            full(g1p), full(g2p), full(w1p), full(w2p), full(a1p), full(a2p),
        ],
        out_specs=[row, row, row, row, row, row2, row2, row2],
        out_shape=[sds((m, GROUP_WIDTH), F32)] * 5 + [sds((2, m, GROUP_WIDTH), F32)] * 3,
        compiler_params=_cparams(("parallel",)),
        name="rwkv_prep",
    )(proj, proj, proj, mu, vecs, dvec, g1p, g2p, w1p, w2p, a1p, a2p)


RW_CHUNK = 64
RW_BATCH_PER_ITER = 2


def _bdot(a, b):
    return jnp.dot(a.astype(BF16), b.astype(BF16), preferred_element_type=F32)


def _bdot_nt(a, b):
    return lax.dot_general(a.astype(BF16), b.astype(BF16), (((1,), (1,)), ((), ())),
                           preferred_element_type=F32)


def _bdot_tn(a, b):
    return lax.dot_general(a.astype(BF16), b.astype(BF16), (((0,), (0,)), ((), ())),
                           preferred_element_type=F32)


def _rwkv_scan_kernel(rf, vf, nf, lf, kf, bf, rb, vb, nb, lb, kb, bb, yf_ref, yb_ref, st_ref,
                      *, bn):
    c = RW_CHUNK
    pw = 2 * RW_HEAD_DIM

    @pl.when(pl.program_id(0) == 0)
    def _():
        st_ref[...] = jnp.zeros_like(st_ref)

    row = lax.broadcasted_iota(jnp.int32, (c, pw), 0)
    col = lax.broadcasted_iota(jnp.int32, (c, pw), 1)
    first = col < RW_HEAD_DIM
    colh = col % RW_HEAD_DIM
    incl2 = {0: (colh <= row).astype(F32), 1: (colh >= row).astype(F32)}
    strict2 = {0: (colh < row).astype(F32), 1: (colh > row).astype(F32)}
    eye2 = (colh == row).astype(F32)
    trow = lax.broadcasted_iota(jnp.int32, (c, c), 0)
    tcol = lax.broadcasted_iota(jnp.int32, (c, c), 1)
    tri = {0: (tcol <= trow).astype(BF16), 1: (tcol >= trow).astype(BF16)}
    br = lax.broadcasted_iota(jnp.int32, (pw, pw), 0) // RW_HEAD_DIM
    bc = lax.broadcasted_iota(jnp.int32, (pw, pw), 1) // RW_HEAD_DIM
    same_head = br == bc

    def bd(x):
        return jnp.concatenate([jnp.where(first, x, 0.0), jnp.where(first, 0.0, x)], axis=0)

    npair = GROUP_WIDTH // pw
    cat = jnp.concatenate

    def body(it, carry):
        ch = []
        chain_inputs = [(it * RW_BATCH_PER_ITER + bi, d, refs)
                        for bi in range(RW_BATCH_PER_ITER)
                        for d, refs in ((0, (rf, vf, nf, lf, kf, bf)), (1, (rb, vb, nb, lb, kb, bb)))]
        for b, d, refs in chain_inputs:
            r_, v_, a_, lw, k_, b_ = (ref[b] for ref in refs)
            hi = lw.astype(BF16)
            r1 = lw - hi.astype(F32)
            mid = r1.astype(BF16)
            lo = (r1 - mid.astype(F32)).astype(BF16)
            cs = (jnp.dot(tri[d], hi, preferred_element_type=F32)
                  + jnp.dot(tri[d], mid, preferred_element_type=F32)
                  + jnp.dot(tri[d], lo, preferred_element_type=F32))
            tot = cs[c - 1:c] if d == 0 else cs[0:1]
            g_incl = jnp.exp(cs)
            inv = jnp.exp(-cs)
            gend = jnp.exp(tot - cs)
            gtot = jnp.exp(tot)
            at = a_ * jnp.exp(cs - lw)
            bt = b_ * inv
            kt = k_ * inv
            rt = r_ * g_incl
            bh = b_ * gend
            kh = k_ * gend
            for p in range(npair):
                sl = slice(p * pw, (p + 1) * pw)
                ch.append(dict(b=b, d=d, idx=(b * 2 + d) * npair + p, vm=v_[:, sl], at=at[:, sl],
                               bt=bt[:, sl], kt=kt[:, sl], rt=rt[:, sl], bh=bh[:, sl],
                               kh=kh[:, sl], gtot=gtot[:, sl]))
        for q in ch:
            o = _bdot_nt(cat([q['at'], q['rt']], axis=0),
                         cat([bd(q['bt']), bd(q['kt'])], axis=0))
            q['a_ab'] = o[:c, :pw] * strict2[q['d']]
            q['a_ak'] = o[:c, pw:] * strict2[q['d']]
            q['q_bk'] = cat([o[c:, :pw] * incl2[q['d']], o[c:, pw:] * incl2[q['d']]], axis=1)
            q['tm'] = eye2 + q['a_ab']
        for q in ch:
            q['pp'] = _bdot(q['a_ab'], bd(q['a_ab']))
        for i in range(5):
            for q in ch:
                rr = _bdot(cat([q['tm'], q['pp']], axis=0), bd(q['pp']))
                q['tm'] = q['tm'] + rr[:c]
                q['pp'] = rr[c:]
        for q in ch:
            q['akv'] = _bdot(q['a_ak'], bd(q['vm']))
        for q in ch:
            q['w_uv'] = _bdot(q['tm'], cat([bd(q['at']), bd(q['akv'])], axis=1))
        for q in ch:
            q['s0'] = st_ref[q['idx']]
            q['uy'] = _bdot_nt(cat([q['w_uv'][:, :pw], q['rt']], axis=0), q['s0'])
        for q in ch:
            q['u'] = q['uy'][:c] + q['w_uv'][:, pw:]
            upd = _bdot_tn(cat([q['u'], q['vm']], axis=0), cat([q['bh'], q['kh']], axis=0))
            st_ref[q['idx']] = q['s0'] * q['gtot'] + jnp.where(same_head, upd, 0.0)
        for q in ch:
            q['y'] = q['uy'][c:] + _bdot(q['q_bk'], cat([bd(q['u']), bd(q['vm'])], axis=0))
        for k, (b, d, _) in enumerate(chain_inputs):
            y_ref = yf_ref if d == 0 else yb_ref
            y_ref[b] = cat([q['y'] for q in ch[k * npair:(k + 1) * npair]], axis=1)
        return carry

    lax.fori_loop(0, bn // RW_BATCH_PER_ITER, body, 0)


def _rwkv_scan(r, v, nkk, w, kd, b):
    bn, seq, _ = r.shape
    nc = seq // RW_CHUNK
    blk = (bn, RW_CHUNK, GROUP_WIDTH)
    fwd = pl.BlockSpec(blk, lambda c: (0, c, 0))
    bwd = pl.BlockSpec(blk, lambda c: (0, nc - 1 - c, 0))
    fwd_d = pl.BlockSpec((None,) + blk, lambda c: (0, 0, c, 0))
    bwd_d = pl.BlockSpec((None,) + blk, lambda c: (1, 0, nc - 1 - c, 0))
    return pl.pallas_call(
        functools.partial(_rwkv_scan_kernel, bn=bn),
        grid=(nc,),
        in_specs=[fwd, fwd, fwd, fwd_d, fwd_d, fwd_d, bwd, bwd, bwd, bwd_d, bwd_d, bwd_d],
        out_specs=[fwd, bwd],
        out_shape=[jax.ShapeDtypeStruct((bn, seq, GROUP_WIDTH), F32)] * 2,
        scratch_shapes=[pltpu.VMEM((2 * bn * (RW_HEADS // 2), 2 * RW_HEAD_DIM, 2 * RW_HEAD_DIM), F32)],
        compiler_params=_cparams(("arbitrary",)),
        name="rwkv_scan",
    )(r, v, nkk, w, kd, b, r, v, nkk, w, kd, b)


def _rwkv_post_kernel(yf_ref, yb_ref, bonus_ref, g_ref, lnw_ref, lnb_ref, gain_ref, o_ref):
    ones = _head_ones(GROUP_WIDTH)
    y = yf_ref[...] + yb_ref[...]
    mean = _head_sum2(y, ones) * (1.0 / RW_HEAD_DIM)
    yc = y - mean
    var = _head_sum2(yc * yc, ones) * (1.0 / RW_HEAD_DIM)
    yn = yc * lax.rsqrt(var + RW_GN_EPS) * lnw_ref[...] + lnb_ref[...]
    out = (yn + bonus_ref[...]) * g_ref[...]
    o_ref[...] = _rms(out, gain_ref[...]).astype(BF16)


def _rwkv_post(yf, yb, bonus, g, ln_w, ln_b, gain, *, tm=512):
    m = yf.shape[0]
    row = pl.BlockSpec((tm, GROUP_WIDTH), lambda i: (i, 0))
    vec = pl.BlockSpec((1, GROUP_WIDTH), lambda i: (0, 0))
    v2 = lambda a: a.reshape(1, GROUP_WIDTH)
    return pl.pallas_call(
        _rwkv_post_kernel,
        grid=(m // tm,),
        in_specs=[row, row, row, row, vec, vec, vec],
        out_specs=row,
        out_shape=jax.ShapeDtypeStruct((m, GROUP_WIDTH), BF16),
        compiler_params=_cparams(("parallel",)),
        name="rwkv_post",
    )(yf, yb, bonus, g, v2(ln_w), v2(ln_b), v2(gain))


def _norm_rows_kernel(x_ref, g_ref, o_ref):
    o_ref[...] = _rms(x_ref[...], g_ref[...]).astype(BF16)


def _norm_rows(x, gain, *, tm=512):
    m, n = x.shape
    return pl.pallas_call(
        _norm_rows_kernel,
        grid=(m // tm,),
        in_specs=[pl.BlockSpec((tm, n), lambda i: (i, 0)), pl.BlockSpec((1, n), lambda i: (0, 0))],
        out_specs=pl.BlockSpec((tm, n), lambda i: (i, 0)),
        out_shape=jax.ShapeDtypeStruct((m, n), BF16),
        compiler_params=_cparams(("parallel",)),
        name="norm_rows",
    )(x, gain.reshape(1, n))


def _out_proj_kernel(ya_ref, yb_ref, yc_ref, yd_ref, w_ref, x_ref, o_ref):
    acc = x_ref[...]
    for g, y_ref in enumerate((ya_ref, yb_ref, yc_ref, yd_ref)):
        acc = acc + jnp.dot(y_ref[...], w_ref[g * GROUP_WIDTH:(g + 1) * GROUP_WIDTH, :],
                            preferred_element_type=F32)
    o_ref[...] = acc


def _out_proj(ya, yb, yc, yd, w_out, x, *, tm=512, tn=1024):
    m, n = x.shape
    row = pl.BlockSpec((tm, GROUP_WIDTH), lambda i, j: (i, 0))
    return pl.pallas_call(
        _out_proj_kernel,
        grid=(m // tm, n // tn),
        in_specs=[row, row, row, row,
                  pl.BlockSpec((N_MIXERS * GROUP_WIDTH, tn), lambda i, j: (0, j)),
                  pl.BlockSpec((tm, tn), lambda i, j: (i, j))],
        out_specs=pl.BlockSpec((tm, tn), lambda i, j: (i, j)),
        out_shape=jax.ShapeDtypeStruct((m, n), F32),
        compiler_params=_cparams(("parallel", "parallel")),
        name="out_proj",
    )(ya, yb, yc, yd, w_out.astype(BF16), x)


PEER_TOK = 256
NEG_INF = float("-inf")


def _peer_stats_kernel(q_ref, khi_ref, klo_ref, tau_o, e1_o, e2_o,
                       sc1_ref, sc2_ref, top_ref, cand_ref):
    nt = q_ref.shape[0]
    for h in range(PEER_HEADS):
        qh = q_ref[:, h * LANES:(h + 1) * LANES]
        qhi, qlo = _split2(qh)
        nt_dot = lambda a, b: lax.dot_general(a, b, (((1,), (1,)), ((), ())),
                                              preferred_element_type=F32)
        for p in range(2):
            sc = (nt_dot(khi_ref[h, p], qhi) + nt_dot(khi_ref[h, p], qlo)
                  + nt_dot(klo_ref[h, p], qhi))
            if p == 0:
                sc1_ref[...] = sc
            else:
                sc2_ref[...] = sc
            x = sc
            for i in range(PEER_TOPK):
                m = jnp.max(x, axis=0, keepdims=True)
                top_ref[p, i:i + 1, :] = m
                x = jnp.where(x == m, NEG_INF, x)
        s2 = top_ref[1]
        sub = lax.broadcasted_iota(jnp.int32, (8, nt), 0)
        cand_ref[0:16, :] = top_ref[0, 0:1, :] + s2
        cand_ref[16:24, :] = top_ref[0, 1:2, :] + s2[0:8]
        for g, (i, nj) in enumerate(((2, 5), (3, 4), (4, 3), (5, 2), (6, 2), (7, 2))):
            cand_ref[24 + 8 * g:32 + 8 * g, :] = jnp.where(sub < nj, top_ref[0, i:i + 1, :] + s2[0:8],
                                                          NEG_INF)
        cand_ref[72:80, :] = top_ref[0, 8:16, :] + s2[0:1]
        x = cand_ref[...]
        top = top_ref[0, 0:1, :] + top_ref[1, 0:1, :]
        zsum = jnp.zeros((1, nt), F32)
        m = top
        for i in range(PEER_TOPK):
            m = jnp.max(x, axis=0, keepdims=True)
            zsum = zsum + jnp.exp(m - top)
            x = jnp.where(x == m, NEG_INF, x)
        thr = m
        sc1 = sc1_ref[...]
        th = jnp.full(sc1.shape, jnp.inf, F32)
        for i in range(PEER_TOPK):
            s1i = top_ref[0, i:i + 1, :]
            row_th = jnp.min(jnp.where(s1i + s2 >= thr, s2, jnp.inf), axis=0, keepdims=True)
            th = jnp.where(sc1 == s1i, row_th, th)
        e1 = jnp.exp(sc1 - top_ref[0, 0:1, :]) / zsum
        e2 = jnp.exp(sc2_ref[...] - top_ref[1, 0:1, :])
        tau = jnp.exp(th - top_ref[1, 0:1, :])
        for t in range(nt // LANES):
            tl = slice(t * LANES, (t + 1) * LANES)
            tau_o[h, t] = tau[:, tl]
            e1_o[h, t] = e1[:, tl]
            e2_o[h, t] = e2[:, tl]


def _peer_stats(q, sub_keys):
    ntok = q.shape[0]
    keys = jnp.zeros((PEER_HEADS, 2, PEER_N_KEYS, LANES), F32)
    keys = keys.at[:, 0, :, :PEER_HALF].set(sub_keys[:, 0]).at[:, 1, :, PEER_HALF:].set(sub_keys[:, 1])
    khi = keys.astype(BF16)
    klo = (keys - khi.astype(F32)).astype(BF16)
    kspec = pl.BlockSpec(keys.shape, lambda i: (0, 0, 0, 0))
    big = pl.BlockSpec((PEER_HEADS, PEER_TOK // LANES, PEER_N_KEYS, LANES), lambda i: (0, i, 0, 0))
    return pl.pallas_call(
        _peer_stats_kernel,
        grid=(ntok // PEER_TOK,),
        in_specs=[pl.BlockSpec((PEER_TOK, PEER_HEADS * LANES), lambda i: (i, 0)), kspec, kspec],
        out_specs=[big] * 3,
        out_shape=[jax.ShapeDtypeStruct((PEER_HEADS, ntok // LANES, PEER_N_KEYS, LANES), F32)] * 3,
        scratch_shapes=[pltpu.VMEM((PEER_N_KEYS, PEER_TOK), F32),
                        pltpu.VMEM((PEER_N_KEYS, PEER_TOK), F32),
                        pltpu.VMEM((2, PEER_TOPK, PEER_TOK), F32),
                        pltpu.VMEM((80, PEER_TOK), F32)],
        compiler_params=_cparams(("parallel",)),
        name="peer_stats",
    )(q, khi, klo)


PEER_TB = 512
PEER_ET = 1024


PEER_TC = 256


def _peer_act_kernel(hn_ref, u_ref, g_ref):
    act = lax.dot_general(u_ref[...], hn_ref[...], (((1,), (1,)), ((), ())),
                          preferred_element_type=F32)
    g_ref[...] = (0.5 * act * (1.0 + lax.erf(act * (1.0 / math.sqrt(2.0))))).astype(BF16)


def _peer_act(hn, u):
    ntok, d = hn.shape
    nexp = u.shape[0]
    return pl.pallas_call(
        _peer_act_kernel,
        grid=(ntok // PEER_TB, nexp // PEER_ET),
        in_specs=[pl.BlockSpec((PEER_TB, d), lambda i, j: (i, 0)),
                  pl.BlockSpec((PEER_ET, d), lambda i, j: (j, 0))],
        out_specs=pl.BlockSpec((None, PEER_ET, PEER_TB), lambda i, j: (i, j, 0)),
        out_shape=jax.ShapeDtypeStruct((ntok // PEER_TB, nexp, PEER_TB), BF16),
        compiler_params=_cparams(("parallel", "parallel")),
        name="peer_act",
    )(hn, u)


def _peer_mix_kernel(vt_ref, g_ref, tau_ref, e1_ref, e2_ref, x_ref, o_ref, acc_ref, w_ref):
    j = pl.program_id(1)
    a0 = j * (PEER_ET // PEER_N_KEYS)

    @pl.when(j == 0)
    def _():
        acc_ref[...] = jnp.zeros_like(acc_ref)

    for c in range(PEER_TB // PEER_TC):
        cl = slice(c * PEER_TC, (c + 1) * PEER_TC)
        for t in range(c * PEER_TC // LANES, (c + 1) * PEER_TC // LANES):
            tl = slice(t * LANES, (t + 1) * LANES)
            for ai in range(PEER_ET // PEER_N_KEYS):
                gate = jnp.zeros((PEER_N_KEYS, LANES), F32)
                for h in range(PEER_HEADS):
                    tau = tau_ref[h, t, pl.ds(a0 + ai, 1), :]
                    e1 = e1_ref[h, t, pl.ds(a0 + ai, 1), :]
                    e2 = e2_ref[h, t]
                    gate = gate + jnp.where(e2 >= tau, e2, 0.0) * e1
                rows = slice(ai * PEER_N_KEYS, (ai + 1) * PEER_N_KEYS)
                w_ref[rows, tl] = (gate * g_ref[rows, tl].astype(F32)).astype(BF16)
        acc_ref[:, cl] += jnp.dot(vt_ref[...], w_ref[:, cl], preferred_element_type=F32)

    @pl.when(j == pl.num_programs(1) - 1)
    def _():
        o_ref[...] = x_ref[...] + acc_ref[...].T


def _peer_mix(vt, g, tau, e1, e2, x):
    ntok, d = x.shape
    nexp = vt.shape[0] * PEER_ET
    stat = pl.BlockSpec((PEER_HEADS, PEER_TB // LANES, PEER_N_KEYS, LANES), lambda i, j: (0, i, 0, 0))
    tok = pl.BlockSpec((PEER_TB, d), lambda i, j: (i, 0))
    return pl.pallas_call(
        _peer_mix_kernel,
        grid=(ntok // PEER_TB, nexp // PEER_ET),
        in_specs=[pl.BlockSpec((None, d, PEER_ET), lambda i, j: (j, 0, 0)),
                  pl.BlockSpec((None, PEER_ET, PEER_TB), lambda i, j: (i, j, 0)),
                  stat, stat, stat, tok],
        out_specs=tok,
        out_shape=jax.ShapeDtypeStruct((ntok, d), F32),
        scratch_shapes=[pltpu.VMEM((d, PEER_TB), F32), pltpu.VMEM((PEER_ET, PEER_TB), BF16)],
        compiler_params=_cparams(("parallel", "arbitrary")),
        name="peer_mix",
    )(vt, g, tau, e1, e2, x)


def _peer_ffn(x, norm_gain, w_query, sub_keys, expert_u, expert_v):
    q, hn = _norm_matmul(x, norm_gain, w_query.astype(BF16))
    tau, e1, e2 = _peer_stats(q, sub_keys)
    g = _peer_act(hn, expert_u.astype(BF16))
    nexp, d = expert_v.shape
    vt = expert_v.astype(BF16).reshape(nexp // PEER_ET, PEER_ET, d).transpose(0, 2, 1)
    return _peer_mix(vt, g, tau, e1, e2, x)


def kernel(x, norm_mix, w_in, fourier_w, s5_lam_re, s5_lam_im, s5_log_step, s5_b_re, s5_b_im, s5_c_re, s5_c_im, s5_d, s5_w_glu, attn_q_norm, attn_k_norm, rwkv_mu, rwkv_w0, rwkv_w1, rwkv_w2, rwkv_a0, rwkv_a1, rwkv_a2, rwkv_g1, rwkv_g2, rwkv_k_k, rwkv_k_a, rwkv_r_k, rwkv_ln_w, rwkv_ln_b, branch_norm, w_out, norm_ffn, peer_w_query, peer_sub_keys, peer_u, peer_v):
    bn, seq, d = x.shape
    m = bn * seq
    depth = w_in.shape[0]
    dft_s, dft_c = _dft_tables(seq)
    cc, ss = _rope_tables(seq)
    xf = x.reshape(m, d)
    for l in range(depth):
        bgain = branch_norm[l].reshape(N_MIXERS, GROUP_WIDTH)
        proj, _ = _norm_matmul(xf, norm_mix[l], w_in[l].astype(BF16), tm=1024)
        proj3 = proj.reshape(bn, seq, PROJ_WIDTH)

        ya = _fnet_mix(proj3, dft_s, dft_c, fourier_w[l], bgain[0]).reshape(m, GROUP_WIDTH)

        b_blk, c_blk, lam1, pw = _s5_params(s5_lam_re[l], s5_lam_im[l], s5_log_step[l],
                                            s5_b_re[l], s5_b_im[l], s5_c_re[l], s5_c_im[l])
        u_s5 = _s5_row_order(proj3[:, :, GROUP_WIDTH:2 * GROUP_WIDTH])
        yf, yb = _s5_scan(u_s5, b_blk, c_blk, lam1, pw)
        ybm = _s5_tail(yf.reshape(m, GROUP_WIDTH), yb.reshape(m, GROUP_WIDTH),
                       u_s5.reshape(m, GROUP_WIDTH), s5_d[l], s5_w_glu[l], bgain[1])
        ybm = _s5_row_order(ybm.reshape(bn, seq, GROUP_WIDTH), inverse=True).reshape(m, GROUP_WIDTH)

        att = _attention_mix(proj3, attn_q_norm[l], attn_k_norm[l], cc, ss)
        yc = _norm_rows(att.reshape(m, GROUP_WIDTH), bgain[2])

        r, v, nkk, g, bonus, w, kd, b = _rwkv_prep(
            proj, seq, rwkv_mu[l], rwkv_w0[l], rwkv_w1[l], rwkv_w2[l], rwkv_a0[l], rwkv_a1[l],
            rwkv_a2[l], rwkv_g1[l], rwkv_g2[l], rwkv_k_k[l], rwkv_k_a[l], rwkv_r_k[l])
        s3 = lambda a: a.reshape(bn, seq, GROUP_WIDTH)
        s4 = lambda a: a.reshape(2, bn, seq, GROUP_WIDTH)
        y_f, y_b = _rwkv_scan(s3(r), s3(v), s3(nkk), s4(w), s4(kd), s4(b))
        yd = _rwkv_post(y_f.reshape(m, GROUP_WIDTH), y_b.reshape(m, GROUP_WIDTH), bonus, g,
                        rwkv_ln_w[l], rwkv_ln_b[l], bgain[3])

        xf = _out_proj(ya, ybm, yc, yd, w_out[l], xf)
        xf = _peer_ffn(xf, norm_ffn[l], peer_w_query[l], peer_sub_keys[l], peer_u[l], peer_v[l])
    return xf.reshape(bn, seq, d)
```

```python
import functools
import math

import jax
import jax.numpy as jnp
import numpy as np
from jax import lax
from jax.experimental import pallas as pl
from jax.experimental.pallas import tpu as pltpu

F32 = jnp.float32
BF16 = jnp.bfloat16

D_MODEL = 2048
GROUP_WIDTH = 512
N_MIXERS = 4
FN_HEADS = 4
FN_HEAD_DIM = 128
S5_GROUP_CH = 16
S5_GROUPS = 32
S5_STATE = 64
S5_NSTATE = S5_GROUPS * S5_STATE
HEAD_DIM = 128
ATT_HEADS = 4
KV_HEADS = 2
GRID_W = 64
ROPE_THETA = 10000.0
AXIS_ROT_DIM = 64
RW_HEAD_DIM = 64
RW_HEADS = 8
RW_GN_EPS = 64e-5
LORA_PAD = 128
PEER_HEADS = 8
PEER_HALF = 64
PEER_N_KEYS = 128
PEER_TOPK = 16
NORM_EPS = 1e-6
PROJ_WIDTH = 4096

LANES = 128
VMEM_LIMIT = 56 * 1024 * 1024


def _cparams(sem):
    return pltpu.CompilerParams(dimension_semantics=sem, vmem_limit_bytes=VMEM_LIMIT)


def _rms(x, gain):
    return x * lax.rsqrt(jnp.mean(x * x, axis=-1, keepdims=True) + NORM_EPS) * gain


def _split2(x):
    hi = x.astype(BF16)
    lo = (x - hi.astype(F32)).astype(BF16)
    return hi, lo


def _norm_matmul_kernel(x_ref, g_ref, w_ref, o_ref, *rest):
    xn_ref = rest[-1]

    @pl.when(pl.program_id(1) == 0)
    def _():
        xn = _rms(x_ref[...], g_ref[...])
        xn_ref[...] = xn.astype(BF16)
        if len(rest) == 2:
            rest[0][...] = xn.T.astype(BF16)

    o_ref[...] = jnp.dot(xn_ref[...], w_ref[...], preferred_element_type=F32)


def _norm_matmul(x, gain, w, *, tm=512, tn=1024, emit_xn_t=False):
    m, k = x.shape
    n = w.shape[1]
    out_specs = [pl.BlockSpec((tm, tn), lambda i, j: (i, j))]
    out_shape = [jax.ShapeDtypeStruct((m, n), F32)]
    if emit_xn_t:
        out_specs.append(pl.BlockSpec((k, tm), lambda i, j: (0, i)))
        out_shape.append(jax.ShapeDtypeStruct((k, m), BF16))
    out = pl.pallas_call(
        _norm_matmul_kernel,
        grid=(m // tm, n // tn),
        in_specs=[
            pl.BlockSpec((tm, k), lambda i, j: (i, 0)),
            pl.BlockSpec((1, k), lambda i, j: (0, 0)),
            pl.BlockSpec((k, tn), lambda i, j: (0, j)),
        ],
        out_specs=out_specs,
        out_shape=out_shape,
        scratch_shapes=[pltpu.VMEM((tm, k), BF16)],
        compiler_params=_cparams(("parallel", "arbitrary")),
        name="norm_matmul",
    )(x, gain.reshape(1, k), w)
    return out if emit_xn_t else out[0]


def _fnet_kernel(pa_ref, dftc_ref, dfts_ref, wf_ref, g_ref, o_ref, z_ref, *, seq):
    @pl.when(pl.program_id(1) == 0)
    def _():
        for h in range(FN_HEADS):
            sl = slice(h * FN_HEAD_DIM, (h + 1) * FN_HEAD_DIM)
            xh = pa_ref[0, :, sl].astype(BF16)
            t = jnp.dot(xh, dftc_ref[...], preferred_element_type=F32)
            z_ref[0:seq, sl] = t[:, :FN_HEAD_DIM].astype(BF16)
            z_ref[seq:2 * seq, sl] = t[:, FN_HEAD_DIM:].astype(BF16)

    scale = 1.0 / math.sqrt(seq * FN_HEAD_DIM)
    re = jnp.dot(dfts_ref[...], z_ref[...], preferred_element_type=F32) * scale
    ya = jnp.dot(re.astype(BF16), wf_ref[...], preferred_element_type=F32)
    o_ref[0] = _rms(ya, g_ref[...]).astype(BF16)


def _dft_tables(seq):
    def cos_sin(n):
        k = np.arange(n, dtype=np.int64)
        ang = ((k[:, None] * k[None, :]) % n).astype(np.float64) * (2.0 * math.pi / n)
        return np.cos(ang), np.sin(ang)

    cs, sn = cos_sin(seq)
    dft_s = jnp.asarray(np.concatenate([cs, -sn], axis=1).astype(np.float32)).astype(BF16)
    cs, sn = cos_sin(FN_HEAD_DIM)
    dft_c = jnp.asarray(np.concatenate([cs, sn], axis=1).astype(np.float32)).astype(BF16)
    return dft_s, dft_c


def _fnet_mix(proj3, dft_s, dft_c, w_f, gain, *, tm=512):
    bn, seq, _ = proj3.shape
    return pl.pallas_call(
        functools.partial(_fnet_kernel, seq=seq),
        grid=(bn, seq // tm),
        in_specs=[
            pl.BlockSpec((1, seq, GROUP_WIDTH), lambda b, r: (b, 0, 0)),
            pl.BlockSpec((FN_HEAD_DIM, 2 * FN_HEAD_DIM), lambda b, r: (0, 0)),
            pl.BlockSpec((tm, 2 * seq), lambda b, r: (r, 0)),
            pl.BlockSpec((GROUP_WIDTH, GROUP_WIDTH), lambda b, r: (0, 0)),
            pl.BlockSpec((1, GROUP_WIDTH), lambda b, r: (0, 0)),
        ],
        out_specs=pl.BlockSpec((1, tm, GROUP_WIDTH), lambda b, r: (b, r, 0)),
        out_shape=jax.ShapeDtypeStruct((bn, seq, GROUP_WIDTH), BF16),
        scratch_shapes=[pltpu.VMEM((2 * seq, GROUP_WIDTH), BF16)],
        compiler_params=_cparams(("parallel", "arbitrary")),
        name="fnet_mix",
    )(proj3, dft_c, dft_s, w_f.astype(BF16), gain.reshape(1, GROUP_WIDTH))


S5_CHUNK = 256
S5_SEGS = 8
S5_STEPS = S5_CHUNK // S5_SEGS
S5_TILES_PER_PASS = 4
S5_HALF_CH = GROUP_WIDTH // 2


def _s5_params(lam_re, lam_im, log_step, b_re, b_im, c_re, c_im):
    step = jnp.exp(log_step.astype(F32))[..., None]
    lr = lam_re.astype(F32)
    li = lam_im.astype(F32)
    ar = lr * step
    ai = li * step
    mag = jnp.exp(ar)
    lbr = mag * jnp.cos(ai)
    lbi = mag * jnp.sin(ai)
    den = lr * lr + li * li
    qr = ((lbr - 1.0) * lr + lbi * li) / den
    qi = (lbi * lr - (lbr - 1.0) * li) / den
    bbr = qr[..., None] * b_re - qi[..., None] * b_im
    bbi = qr[..., None] * b_im + qi[..., None] * b_re
    eye = jnp.eye(S5_GROUPS, dtype=F32)

    def in_blk(b):
        t = jnp.einsum('dgph,gk->dghkp', b, eye)
        return t.reshape(2, GROUP_WIDTH, S5_NSTATE)

    def out_blk(c):
        t = jnp.einsum('dghp,gk->dgpkh', c, eye)
        return t.reshape(2, S5_NSTATE, GROUP_WIDTH)

    hc, hs = S5_HALF_CH, S5_NSTATE // 2
    halves = range(2)
    bre, bim = in_blk(bbr), in_blk(bbi)
    b_blk = jnp.stack([jnp.concatenate([m[:, r * hc:(r + 1) * hc, r * hs:(r + 1) * hs]
                                        for m in (bre, bim)], axis=2) for r in halves],
                      axis=1).astype(BF16)
    cre, cim = out_blk(c_re.astype(F32)), -out_blk(c_im.astype(F32))
    c_blk = jnp.stack([jnp.concatenate([m[:, r * hs:(r + 1) * hs, r * hc:(r + 1) * hc]
                                        for m in (cre, cim)], axis=1) for r in halves],
                      axis=1).astype(BF16)
    kpow = jnp.arange(1, S5_STEPS + 1, dtype=F32)[None, :, None, None]
    pmag = jnp.exp(ar[:, None] * kpow)
    pang = ai[:, None] * kpow
    pw = jnp.concatenate([(pmag * jnp.cos(pang)).reshape(2, S5_STEPS, S5_NSTATE),
                          (pmag * jnp.sin(pang)).reshape(2, S5_STEPS, S5_NSTATE)], axis=2)
    lam1 = pw[:, 0:1, :]
    pw_tiles = pw.reshape(2, S5_STEPS, 2 * S5_NSTATE // LANES, LANES).transpose(0, 2, 1, 3)
    return b_blk, c_blk, lam1, pw_tiles


def _s5_scan_kernel(uf_ref, ub_ref, bblk_ref, cblk_ref, lam_ref, pw_ref, yf_ref, yb_ref,
                    bu_ref, carry_ref, cin_ref):
    ns = S5_NSTATE

    @pl.when(pl.program_id(1) == 0)
    def _():
        carry_ref[...] = jnp.zeros_like(carry_ref)

    nt = ns // LANES
    th = nt // 2
    for d, u_ref in ((0, uf_ref), (1, ub_ref)):
        u = u_ref[0].astype(BF16)
        for rb in range(2):
            bu = jnp.dot(u[:, rb * S5_HALF_CH:(rb + 1) * S5_HALF_CH], bblk_ref[d, rb],
                         preferred_element_type=F32)
            for k in range(th):
                bu_ref[d, rb * th + k] = bu[:, k * LANES:(k + 1) * LANES]
                bu_ref[d, nt + rb * th + k] = bu[:, (th + k) * LANES:(th + k + 1) * LANES]

    def lanes(q):
        return slice(q * LANES, (q + 1) * LANES)

    for q0 in range(0, nt, S5_TILES_PER_PASS):
        tiles = range(q0, q0 + S5_TILES_PER_PASS)
        lam = {(d, q): (lam_ref[d, :, lanes(q)], lam_ref[d, :, lanes(nt + q)])
               for d in range(2) for q in tiles}

        def step(i, hs, tiles=tiles, lam=lam):
            out = []
            for d in range(2):
                row = i if d == 0 else S5_STEPS - 1 - i
                idx = pl.ds(pl.multiple_of(row * S5_SEGS, S5_SEGS), S5_SEGS)
                for q in tiles:
                    hr, hi = hs[len(out)], hs[len(out) + 1]
                    lr, li = lam[d, q]
                    nr = lr * hr - li * hi + bu_ref[d, q, idx, :]
                    ni = lr * hi + li * hr + bu_ref[d, nt + q, idx, :]
                    bu_ref[d, q, idx, :] = nr
                    bu_ref[d, nt + q, idx, :] = ni
                    out += [nr, ni]
            return tuple(out)

        z = jnp.zeros((S5_SEGS, LANES), F32)
        ends = lax.fori_loop(0, S5_STEPS, step, (z,) * (4 * S5_TILES_PER_PASS))

        cin = {}
        pos = 0
        for d in range(2):
            order = range(S5_SEGS) if d == 0 else range(S5_SEGS - 1, -1, -1)
            for q in tiles:
                er, ei = ends[pos], ends[pos + 1]
                pos += 2
                pr = pw_ref[d, q, S5_STEPS - 1:S5_STEPS, :]
                pi = pw_ref[d, nt + q, S5_STEPS - 1:S5_STEPS, :]
                cr = carry_ref[d, :, lanes(q)]
                ci = carry_ref[d, :, lanes(nt + q)]
                for sg in order:
                    cin_ref[d, sg:sg + 1, lanes(q)] = cr
                    cin_ref[d, sg:sg + 1, lanes(nt + q)] = ci
                    nr = er[sg:sg + 1] + pr * cr - pi * ci
                    ni = ei[sg:sg + 1] + pr * ci + pi * cr
                    cr, ci = nr, ni
                carry_ref[d, :, lanes(q)] = cr
                carry_ref[d, :, lanes(nt + q)] = ci
                cin[d, q] = (cin_ref[d, :, lanes(q)], cin_ref[d, :, lanes(nt + q)])

        def fix(i, c, tiles=tiles, cin=cin):
            for d in range(2):
                row = i if d == 0 else S5_STEPS - 1 - i
                idx = pl.ds(pl.multiple_of(row * S5_SEGS, S5_SEGS), S5_SEGS)
                for q in tiles:
                    pr = pw_ref[d, q, pl.ds(i, 1), :]
                    pi = pw_ref[d, nt + q, pl.ds(i, 1), :]
                    cr, ci = cin[d, q]
                    bu_ref[d, q, idx, :] = bu_ref[d, q, idx, :] + pr * cr - pi * ci
                    bu_ref[d, nt + q, idx, :] = bu_ref[d, nt + q, idx, :] + pr * ci + pi * cr
            return c

        lax.fori_loop(0, S5_STEPS, fix, 0)

    for d, y_ref in ((0, yf_ref), (1, yb_ref)):
        ys = []
        for cb in range(2):
            tiles = [cb * th + k for k in range(th)] + [nt + cb * th + k for k in range(th)]
            hs = jnp.concatenate([bu_ref[d, q].astype(BF16) for q in tiles], axis=1)
            ys.append(jnp.dot(hs, cblk_ref[d, cb], preferred_element_type=F32))
        y_ref[0] = jnp.concatenate(ys, axis=1)


def _s5_row_order(x, inverse=False):
    bn, seq, n = x.shape
    a, b = (S5_STEPS, S5_SEGS) if inverse else (S5_SEGS, S5_STEPS)
    return x.reshape(bn, seq // S5_CHUNK, a, b, n).transpose(0, 1, 3, 2, 4).reshape(bn, seq, n)


def _s5_scan(u, b_blk, c_blk, lam1, pw):
    bn, seq, _ = u.shape
    nc = seq // S5_CHUNK
    ns2 = 2 * S5_NSTATE
    u_blk = (1, S5_CHUNK, GROUP_WIDTH)
    return pl.pallas_call(
        _s5_scan_kernel,
        grid=(bn, nc),
        in_specs=[
            pl.BlockSpec(u_blk, lambda b, c: (b, c, 0)),
            pl.BlockSpec(u_blk, lambda b, c: (b, nc - 1 - c, 0)),
            pl.BlockSpec((2, 2, S5_HALF_CH, S5_NSTATE), lambda b, c: (0, 0, 0, 0)),
            pl.BlockSpec((2, 2, S5_NSTATE, S5_HALF_CH), lambda b, c: (0, 0, 0, 0)),
            pl.BlockSpec((2, 1, ns2), lambda b, c: (0, 0, 0)),
            pl.BlockSpec((2, ns2 // LANES, S5_STEPS, LANES), lambda b, c: (0, 0, 0, 0)),
        ],
        out_specs=[
            pl.BlockSpec(u_blk, lambda b, c: (b, c, 0)),
            pl.BlockSpec(u_blk, lambda b, c: (b, nc - 1 - c, 0)),
        ],
        out_shape=[jax.ShapeDtypeStruct((bn, seq, GROUP_WIDTH), F32)] * 2,
        scratch_shapes=[
            pltpu.VMEM((2, ns2 // LANES, S5_CHUNK, LANES), F32),
            pltpu.VMEM((2, 1, ns2), F32),
            pltpu.VMEM((2, S5_SEGS, ns2), F32),
        ],
        compiler_params=_cparams(("parallel", "arbitrary")),
        name="s5_scan",
    )(u, u, b_blk, c_blk, lam1, pw)


def _s5_tail_kernel(yf_ref, yb_ref, u_ref, d_ref, wg_ref, g_ref, o_ref):
    y = yf_ref[...] + yb_ref[...] + d_ref[...] * u_ref[...]
    y = jax.nn.gelu(y)
    gate = jnp.dot(y.astype(BF16), wg_ref[...], preferred_element_type=F32)
    o_ref[...] = _rms(y * jax.nn.sigmoid(gate), g_ref[...]).astype(BF16)


def _s5_tail(yf, yb, u, d_skip, w_glu, gain, *, tm=512):
    m = yf.shape[0]
    row = pl.BlockSpec((tm, GROUP_WIDTH), lambda i: (i, 0))
    vec = pl.BlockSpec((1, GROUP_WIDTH), lambda i: (0, 0))
    return pl.pallas_call(
        _s5_tail_kernel,
        grid=(m // tm,),
        in_specs=[row, row, row, vec,
                  pl.BlockSpec((GROUP_WIDTH, GROUP_WIDTH), lambda i: (0, 0)), vec],
        out_specs=row,
        out_shape=jax.ShapeDtypeStruct((m, GROUP_WIDTH), BF16),
        compiler_params=_cparams(("parallel",)),
        name="s5_tail",
    )(yf, yb, u, d_skip.reshape(1, GROUP_WIDTH), w_glu.astype(BF16), gain.reshape(1, GROUP_WIDTH))


def _rope(x, cc, ss):
    return x * cc + pltpu.roll(x, HEAD_DIM // 2, axis=1) * ss


def _attn_kernel(q_ref, k_ref, v_ref, qg_ref, kg_ref, ccq_ref, ssq_ref, cck_ref, ssk_ref,
                 o_ref, kp_ref, vp_ref):
    @pl.when(pl.program_id(2) == 0)
    def _():
        kn = _rms(k_ref[0], kg_ref[...])
        kp_ref[...] = _rope(kn, cck_ref[...], ssk_ref[...]).astype(BF16)
        vp_ref[...] = v_ref[0].astype(BF16)

    rep = ATT_HEADS // KV_HEADS
    outs = []
    for r in range(rep):
        q = q_ref[0, :, r * HEAD_DIM:(r + 1) * HEAD_DIM]
        qn = _rope(_rms(q, qg_ref[...]), ccq_ref[...], ssq_ref[...]) * (HEAD_DIM ** -0.5)
        s = lax.dot_general(qn.astype(BF16), kp_ref[...], (((1,), (1,)), ((), ())),
                            preferred_element_type=F32)
        p = jnp.exp(s - jnp.max(s, axis=-1, keepdims=True))
        inv = 1.0 / jnp.sum(p, axis=-1, keepdims=True)
        o = jnp.dot(p.astype(BF16), vp_ref[...], preferred_element_type=F32) * inv
        outs.append(o)
    o_ref[0] = jnp.concatenate(outs, axis=1)


def _rope_tables(seq):
    rows = seq // GRID_W
    row_id = jnp.repeat(jnp.arange(rows), GRID_W).astype(F32)
    col_id = jnp.tile(jnp.arange(GRID_W), rows).astype(F32)
    inv = ROPE_THETA ** (-jnp.arange(0, AXIS_ROT_DIM, 2, dtype=F32) / AXIS_ROT_DIM)
    ang = jnp.concatenate([row_id[:, None] * inv, col_id[:, None] * inv], axis=-1)
    cos, sin = jnp.cos(ang), jnp.sin(ang)
    return jnp.concatenate([cos, cos], axis=1), jnp.concatenate([-sin, sin], axis=1)


def _attention_mix(proj3, q_gain, k_gain, cc, ss, *, tq=256):
    bn, seq, _ = proj3.shape
    rep = ATT_HEADS // KV_HEADS
    qw = rep * HEAD_DIM
    q_col0 = (2 * GROUP_WIDTH) // qw
    k_col0 = (3 * GROUP_WIDTH) // HEAD_DIM
    v_col0 = k_col0 + KV_HEADS
    vec = pl.BlockSpec((1, HEAD_DIM), lambda b, g, i: (0, 0))
    return pl.pallas_call(
        _attn_kernel,
        grid=(bn, KV_HEADS, seq // tq),
        in_specs=[
            pl.BlockSpec((1, tq, qw), lambda b, g, i: (b, i, q_col0 + g)),
            pl.BlockSpec((1, seq, HEAD_DIM), lambda b, g, i: (b, 0, k_col0 + g)),
            pl.BlockSpec((1, seq, HEAD_DIM), lambda b, g, i: (b, 0, v_col0 + g)),
            vec, vec,
            pl.BlockSpec((tq, HEAD_DIM), lambda b, g, i: (i, 0)),
            pl.BlockSpec((tq, HEAD_DIM), lambda b, g, i: (i, 0)),
            pl.BlockSpec((seq, HEAD_DIM), lambda b, g, i: (0, 0)),
            pl.BlockSpec((seq, HEAD_DIM), lambda b, g, i: (0, 0)),
        ],
        out_specs=pl.BlockSpec((1, tq, qw), lambda b, g, i: (b, i, g)),
        out_shape=jax.ShapeDtypeStruct((bn, seq, GROUP_WIDTH), F32),
        scratch_shapes=[pltpu.VMEM((seq, HEAD_DIM), BF16), pltpu.VMEM((seq, HEAD_DIM), BF16)],
        compiler_params=_cparams(("parallel", "parallel", "arbitrary")),
        name="gqa_attention",
    )(proj3, proj3, proj3, q_gain.reshape(1, HEAD_DIM), k_gain.reshape(1, HEAD_DIM),
      cc, ss, cc, ss)


def _head_ones(width):
    r = lax.broadcasted_iota(jnp.int32, (width, width), 0) // RW_HEAD_DIM
    c = lax.broadcasted_iota(jnp.int32, (width, width), 1) // RW_HEAD_DIM
    return (r == c).astype(BF16)


def _head_sum2(x, ones):
    hi, lo = _split2(x)
    return (jnp.dot(hi, ones, preferred_element_type=F32)
            + jnp.dot(lo, ones, preferred_element_type=F32))


def _rwkv_prep_kernel(c_ref, p_ref, n_ref, mu_ref, vecs_ref, dvec_ref,
                      g1_ref, g2_ref, w1_ref, w2_ref, a1_ref, a2_ref,
                      r_o, v_o, nkk_o, g_o, bonus_o, w_o, kd_o, b_o, *, tiles_per_seq):
    i = pl.program_id(0)
    tm = c_ref.shape[0]
    first = (i % tiles_per_seq) == 0
    last = (i % tiles_per_seq) == tiles_per_seq - 1
    rows = lax.broadcasted_iota(jnp.int32, (tm, GROUP_WIDTH), 0)
    ones = _head_ones(GROUP_WIDTH)

    def shifted(j):
        sl = slice(j * GROUP_WIDTH, (j + 1) * GROUP_WIDTH)
        x = c_ref[:, sl]
        prev_row = jnp.where(first, 0.0, p_ref[7:8, sl])
        next_row = jnp.where(last, 0.0, n_ref[0:1, sl])
        prev = jnp.where(rows == 0, prev_row, pltpu.roll(x, 1, axis=0))
        nxt = jnp.where(rows == tm - 1, next_row, pltpu.roll(x, tm - 1, axis=0))
        return x + (prev - x) * mu_ref[j, 0:1, :] + (nxt - x) * mu_ref[j, 1:2, :]

    r, k, v, z = (shifted(j) for j in range(4))
    k_k, k_a, r_k = vecs_ref[0:1, :], vecs_ref[1:2, :], vecs_ref[2:3, :]
    zb = z.astype(BF16)

    gmid = jax.nn.sigmoid(jnp.dot(zb, g1_ref[...], preferred_element_type=F32))
    g_o[...] = jnp.dot(gmid.astype(BF16), g2_ref[...], preferred_element_type=F32)

    kk = k * k_k
    kk = kk * lax.rsqrt(_head_sum2(kk * kk, ones) + 1e-12)
    r_o[...] = r
    v_o[...] = v
    nkk_o[...] = -kk

    bonus = jnp.zeros_like(r)
    for d in range(2):
        w0, a0 = dvec_ref[d, 0:1, :], dvec_ref[d, 1:2, :]
        wmid = jnp.tanh(jnp.dot(zb, w1_ref[d], preferred_element_type=F32))
        wlin = w0 + jnp.dot(wmid.astype(BF16), w2_ref[d], preferred_element_type=F32)
        w_log = -jax.nn.softplus(-wlin) - 0.5
        w_o[d] = -jnp.exp(w_log)
        amid = jnp.dot(zb, a1_ref[d], preferred_element_type=F32)
        a = jax.nn.sigmoid(a0 + jnp.dot(amid.astype(BF16), a2_ref[d], preferred_element_type=F32))
        kd = k * (1.0 + (a - 1.0) * k_a)
        kd_o[d] = kd
        b_o[d] = kk * a
        bonus = bonus + _head_sum2(r * kd * r_k, ones) * v
    bonus_o[...] = bonus


def _pad_cols(w, n):
    return jnp.pad(w, [(0, 0)] * (w.ndim - 1) + [(0, n - w.shape[-1])])


def _pad_rows(w, n):
    return jnp.pad(w, [(0, 0)] * (w.ndim - 2) + [(0, n - w.shape[-2]), (0, 0)])


def _rwkv_prep(proj, seq, mu, w0, w1, w2, a0, a1, a2, g1, g2, k_k, k_a, r_k, *, tm=512):
    m = proj.shape[0]
    wide = 4 * GROUP_WIDTH
    tiles_per_seq = seq // tm
    nblk8 = m // 8
    vecs = jnp.stack([k_k, k_a, r_k.reshape(GROUP_WIDTH)]
                     + [jnp.zeros((GROUP_WIDTH,), F32)] * 5).astype(F32)
    dvec = jnp.stack([jnp.stack([w0[d], a0[d]] + [jnp.zeros((GROUP_WIDTH,), F32)] * 6)
                      for d in range(2)]).astype(F32)
    row = pl.BlockSpec((tm, GROUP_WIDTH), lambda i: (i, 0))
    row2 = pl.BlockSpec((2, tm, GROUP_WIDTH), lambda i: (0, i, 0))
    full = lambda a: pl.BlockSpec(a.shape, lambda i: (0,) * a.ndim)
    g1p = _pad_cols(g1, LORA_PAD).astype(BF16)
    g2p = _pad_rows(g2, LORA_PAD).astype(BF16)
    w1p = _pad_cols(w1, LORA_PAD).astype(BF16)
    w2p = _pad_rows(w2, LORA_PAD).astype(BF16)
    a1p = _pad_cols(a1, LORA_PAD).astype(BF16)
    a2p = _pad_rows(a2, LORA_PAD).astype(BF16)
    sds = jax.ShapeDtypeStruct
    return pl.pallas_call(
        functools.partial(_rwkv_prep_kernel, tiles_per_seq=tiles_per_seq),
        grid=(m // tm,),
        in_specs=[
            pl.BlockSpec((tm, wide), lambda i: (i, 1)),
            pl.BlockSpec((8, wide), lambda i: (jnp.maximum(i * (tm // 8) - 1, 0), 1)),
            pl.BlockSpec((8, wide), lambda i: (jnp.minimum((i + 1) * (tm // 8), nblk8 - 1), 1)),
            full(mu), full(vecs), full(dvec),
            full(g1p), full(g2p), full(w1p), full(w2p), full(a1p), full(a2p),
        ],
        out_specs=[row, row, row, row, row, row2, row2, row2],
        out_shape=[sds((m, GROUP_WIDTH), F32)] * 5 + [sds((2, m, GROUP_WIDTH), F32)] * 3,
        compiler_params=_cparams(("parallel",)),
        name="rwkv_prep",
    )(proj, proj, proj, mu, vecs, dvec, g1p, g2p, w1p, w2p, a1p, a2p)


RW_CHUNK = 64
RW_BATCH_PER_ITER = 4


def _bdot(a, b):
    return jnp.dot(a.astype(BF16), b.astype(BF16), preferred_element_type=F32)


def _bdot_nt(a, b):
    return lax.dot_general(a.astype(BF16), b.astype(BF16), (((1,), (1,)), ((), ())),
                           preferred_element_type=F32)


def _bdot_tn(a, b):
    return lax.dot_general(a.astype(BF16), b.astype(BF16), (((0,), (0,)), ((), ())),
                           preferred_element_type=F32)


def _rwkv_scan_kernel(rf, vf, nf, lf, kf, bf, rb, vb, nb, lb, kb, bb, yf_ref, yb_ref, st_ref,
                      *, bn):
    c = RW_CHUNK
    pw = 2 * RW_HEAD_DIM

    @pl.when(pl.program_id(0) == 0)
    def _():
        st_ref[...] = jnp.zeros_like(st_ref)

    row = lax.broadcasted_iota(jnp.int32, (c, pw), 0)
    col = lax.broadcasted_iota(jnp.int32, (c, pw), 1)
    first = col < RW_HEAD_DIM
    colh = col % RW_HEAD_DIM
    incl2 = {0: (colh <= row).astype(F32), 1: (colh >= row).astype(F32)}
    strict2 = {0: (colh < row).astype(F32), 1: (colh > row).astype(F32)}
    eye2 = (colh == row).astype(F32)
    trow = lax.broadcasted_iota(jnp.int32, (c, c), 0)
    tcol = lax.broadcasted_iota(jnp.int32, (c, c), 1)
    tri = {0: (tcol <= trow).astype(BF16), 1: (tcol >= trow).astype(BF16)}
    br = lax.broadcasted_iota(jnp.int32, (pw, pw), 0) // RW_HEAD_DIM
    bc = lax.broadcasted_iota(jnp.int32, (pw, pw), 1) // RW_HEAD_DIM
    same_head = br == bc

    def bd(x):
        return jnp.concatenate([jnp.where(first, x, 0.0), jnp.where(first, 0.0, x)], axis=0)

    npair = GROUP_WIDTH // pw
    cat = jnp.concatenate

    def body(it, carry):
        ch = []
        chain_inputs = [(it * RW_BATCH_PER_ITER + bi, d, refs)
                        for bi in range(RW_BATCH_PER_ITER)
                        for d, refs in ((0, (rf, vf, nf, lf, kf, bf)), (1, (rb, vb, nb, lb, kb, bb)))]
        for b, d, refs in chain_inputs:
            r_, v_, a_, lw, k_, b_ = (ref[b] for ref in refs)
            hi = lw.astype(BF16)
            r1 = lw - hi.astype(F32)
            mid = r1.astype(BF16)
            lo = (r1 - mid.astype(F32)).astype(BF16)
            cs = (jnp.dot(tri[d], hi, preferred_element_type=F32)
                  + jnp.dot(tri[d], mid, preferred_element_type=F32)
                  + jnp.dot(tri[d], lo, preferred_element_type=F32))
            tot = cs[c - 1:c] if d == 0 else cs[0:1]
            g_incl = jnp.exp(cs)
            inv = jnp.exp(-cs)
            gend = jnp.exp(tot - cs)
            gtot = jnp.exp(tot)
            at = a_ * jnp.exp(cs - lw)
            bt = b_ * inv
            kt = k_ * inv
            rt = r_ * g_incl
            bh = b_ * gend
            kh = k_ * gend
            for p in range(npair):
                sl = slice(p * pw, (p + 1) * pw)
                ch.append(dict(b=b, d=d, idx=(b * 2 + d) * npair + p, vm=v_[:, sl], at=at[:, sl],
                               bt=bt[:, sl], kt=kt[:, sl], rt=rt[:, sl], bh=bh[:, sl],
                               kh=kh[:, sl], gtot=gtot[:, sl]))
        for q in ch:
            o = _bdot_nt(cat([q['at'], q['rt']], axis=0),
                         cat([bd(q['bt']), bd(q['kt'])], axis=0))
            q['a_ab'] = o[:c, :pw] * strict2[q['d']]
            q['a_ak'] = o[:c, pw:] * strict2[q['d']]
            q['q_bk'] = cat([o[c:, :pw] * incl2[q['d']], o[c:, pw:] * incl2[q['d']]], axis=1)
            q['tm'] = eye2 + q['a_ab']
        for q in ch:
            q['pp'] = _bdot(q['a_ab'], bd(q['a_ab']))
        for i in range(5):
            for q in ch:
                rr = _bdot(cat([q['tm'], q['pp']], axis=0), bd(q['pp']))
                q['tm'] = q['tm'] + rr[:c]
                q['pp'] = rr[c:]
        for q in ch:
            q['akv'] = _bdot(q['a_ak'], bd(q['vm']))
        for q in ch:
            q['w_uv'] = _bdot(q['tm'], cat([bd(q['at']), bd(q['akv'])], axis=1))
        for q in ch:
            q['s0'] = st_ref[q['idx']]
            q['uy'] = _bdot_nt(cat([q['w_uv'][:, :pw], q['rt']], axis=0), q['s0'])
        for q in ch:
            q['u'] = q['uy'][:c] + q['w_uv'][:, pw:]
            upd = _bdot_tn(cat([q['u'], q['vm']], axis=0), cat([q['bh'], q['kh']], axis=0))
            st_ref[q['idx']] = q['s0'] * q['gtot'] + jnp.where(same_head, upd, 0.0)
        for q in ch:
            q['y'] = q['uy'][c:] + _bdot(q['q_bk'], cat([bd(q['u']), bd(q['vm'])], axis=0))
        for k, (b, d, _) in enumerate(chain_inputs):
            y_ref = yf_ref if d == 0 else yb_ref
            y_ref[b] = cat([q['y'] for q in ch[k * npair:(k + 1) * npair]], axis=1)
        return carry

    lax.fori_loop(0, bn // RW_BATCH_PER_ITER, body, 0)


def _rwkv_scan(r, v, nkk, w, kd, b):
    bn, seq, _ = r.shape
    nc = seq // RW_CHUNK
    blk = (bn, RW_CHUNK, GROUP_WIDTH)
    fwd = pl.BlockSpec(blk, lambda c: (0, c, 0))
    bwd = pl.BlockSpec(blk, lambda c: (0, nc - 1 - c, 0))
    fwd_d = pl.BlockSpec((None,) + blk, lambda c: (0, 0, c, 0))
    bwd_d = pl.BlockSpec((None,) + blk, lambda c: (1, 0, nc - 1 - c, 0))
    return pl.pallas_call(
        functools.partial(_rwkv_scan_kernel, bn=bn),
        grid=(nc,),
        in_specs=[fwd, fwd, fwd, fwd_d, fwd_d, fwd_d, bwd, bwd, bwd, bwd_d, bwd_d, bwd_d],
        out_specs=[fwd, bwd],
        out_shape=[jax.ShapeDtypeStruct((bn, seq, GROUP_WIDTH), F32)] * 2,
        scratch_shapes=[pltpu.VMEM((2 * bn * (RW_HEADS // 2), 2 * RW_HEAD_DIM, 2 * RW_HEAD_DIM), F32)],
        compiler_params=_cparams(("arbitrary",)),
        name="rwkv_scan",
    )(r, v, nkk, w, kd, b, r, v, nkk, w, kd, b)


def _rwkv_post_kernel(yf_ref, yb_ref, bonus_ref, g_ref, lnw_ref, lnb_ref, gain_ref, o_ref):
    ones = _head_ones(GROUP_WIDTH)
    y = yf_ref[...] + yb_ref[...]
    mean = _head_sum2(y, ones) * (1.0 / RW_HEAD_DIM)
    yc = y - mean
    var = _head_sum2(yc * yc, ones) * (1.0 / RW_HEAD_DIM)
    yn = yc * lax.rsqrt(var + RW_GN_EPS) * lnw_ref[...] + lnb_ref[...]
    out = (yn + bonus_ref[...]) * g_ref[...]
    o_ref[...] = _rms(out, gain_ref[...]).astype(BF16)


def _rwkv_post(yf, yb, bonus, g, ln_w, ln_b, gain, *, tm=512):
    m = yf.shape[0]
    row = pl.BlockSpec((tm, GROUP_WIDTH), lambda i: (i, 0))
    vec = pl.BlockSpec((1, GROUP_WIDTH), lambda i: (0, 0))
    v2 = lambda a: a.reshape(1, GROUP_WIDTH)
    return pl.pallas_call(
        _rwkv_post_kernel,
        grid=(m // tm,),
        in_specs=[row, row, row, row, vec, vec, vec],
        out_specs=row,
        out_shape=jax.ShapeDtypeStruct((m, GROUP_WIDTH), BF16),
        compiler_params=_cparams(("parallel",)),
        name="rwkv_post",
    )(yf, yb, bonus, g, v2(ln_w), v2(ln_b), v2(gain))


def _norm_rows_kernel(x_ref, g_ref, o_ref):
    o_ref[...] = _rms(x_ref[...], g_ref[...]).astype(BF16)


def _norm_rows(x, gain, *, tm=512):
    m, n = x.shape
    return pl.pallas_call(
        _norm_rows_kernel,
        grid=(m // tm,),
        in_specs=[pl.BlockSpec((tm, n), lambda i: (i, 0)), pl.BlockSpec((1, n), lambda i: (0, 0))],
        out_specs=pl.BlockSpec((tm, n), lambda i: (i, 0)),
        out_shape=jax.ShapeDtypeStruct((m, n), BF16),
        compiler_params=_cparams(("parallel",)),
        name="norm_rows",
    )(x, gain.reshape(1, n))


def _out_proj_kernel(ya_ref, yb_ref, yc_ref, yd_ref, w_ref, x_ref, o_ref):
    acc = x_ref[...]
    for g, y_ref in enumerate((ya_ref, yb_ref, yc_ref, yd_ref)):
        acc = acc + jnp.dot(y_ref[...], w_ref[g * GROUP_WIDTH:(g + 1) * GROUP_WIDTH, :],
                            preferred_element_type=F32)
    o_ref[...] = acc


def _out_proj(ya, yb, yc, yd, w_out, x, *, tm=512, tn=1024):
    m, n = x.shape
    row = pl.BlockSpec((tm, GROUP_WIDTH), lambda i, j: (i, 0))
    return pl.pallas_call(
        _out_proj_kernel,
        grid=(m // tm, n // tn),
        in_specs=[row, row, row, row,
                  pl.BlockSpec((N_MIXERS * GROUP_WIDTH, tn), lambda i, j: (0, j)),
                  pl.BlockSpec((tm, tn), lambda i, j: (i, j))],
        out_specs=pl.BlockSpec((tm, tn), lambda i, j: (i, j)),
        out_shape=jax.ShapeDtypeStruct((m, n), F32),
        compiler_params=_cparams(("parallel", "parallel")),
        name="out_proj",
    )(ya, yb, yc, yd, w_out.astype(BF16), x)


PEER_TOK = 256
NEG_INF = float("-inf")


def _peer_stats_kernel(q_ref, khi_ref, klo_ref, tau_o, e1_o, e2_o,
                       sc1_ref, sc2_ref, top_ref, cand_ref):
    nt = q_ref.shape[0]
    for h in range(PEER_HEADS):
        qh = q_ref[:, h * LANES:(h + 1) * LANES]
        qhi, qlo = _split2(qh)
        nt_dot = lambda a, b: lax.dot_general(a, b, (((1,), (1,)), ((), ())),
                                              preferred_element_type=F32)
        for p in range(2):
            sc = (nt_dot(khi_ref[h, p], qhi) + nt_dot(khi_ref[h, p], qlo)
                  + nt_dot(klo_ref[h, p], qhi))
            if p == 0:
                sc1_ref[...] = sc
            else:
                sc2_ref[...] = sc
            x = sc
            for i in range(PEER_TOPK):
                m = jnp.max(x, axis=0, keepdims=True)
                top_ref[p, i:i + 1, :] = m
                x = jnp.where(x == m, NEG_INF, x)
        s2 = top_ref[1]
        sub = lax.broadcasted_iota(jnp.int32, (8, nt), 0)
        cand_ref[0:16, :] = top_ref[0, 0:1, :] + s2
        cand_ref[16:24, :] = top_ref[0, 1:2, :] + s2[0:8]
        for g, (i, nj) in enumerate(((2, 5), (3, 4), (4, 3), (5, 2), (6, 2), (7, 2))):
            cand_ref[24 + 8 * g:32 + 8 * g, :] = jnp.where(sub < nj, top_ref[0, i:i + 1, :] + s2[0:8],
                                                          NEG_INF)
        cand_ref[72:80, :] = top_ref[0, 8:16, :] + s2[0:1]
        x = cand_ref[...]
        top = top_ref[0, 0:1, :] + top_ref[1, 0:1, :]
        zsum = jnp.zeros((1, nt), F32)
        m = top
        for i in range(PEER_TOPK):
            m = jnp.max(x, axis=0, keepdims=True)
            zsum = zsum + jnp.exp(m - top)
            x = jnp.where(x == m, NEG_INF, x)
        thr = m
        sc1 = sc1_ref[...]
        th = jnp.full(sc1.shape, jnp.inf, F32)
        for i in range(PEER_TOPK):
            s1i = top_ref[0, i:i + 1, :]
            row_th = jnp.min(jnp.where(s1i + s2 >= thr, s2, jnp.inf), axis=0, keepdims=True)
            th = jnp.where(sc1 == s1i, row_th, th)
        e1 = jnp.exp(sc1 - top_ref[0, 0:1, :]) / zsum
        e2 = jnp.exp(sc2_ref[...] - top_ref[1, 0:1, :])
        tau = jnp.exp(th - top_ref[1, 0:1, :])
        for t in range(nt // LANES):
            tl = slice(t * LANES, (t + 1) * LANES)
            tau_o[h, t] = tau[:, tl]
            e1_o[h, t] = e1[:, tl]
            e2_o[h, t] = e2[:, tl]


def _peer_stats(q, sub_keys):
    ntok = q.shape[0]
    keys = jnp.zeros((PEER_HEADS, 2, PEER_N_KEYS, LANES), F32)
    keys = keys.at[:, 0, :, :PEER_HALF].set(sub_keys[:, 0]).at[:, 1, :, PEER_HALF:].set(sub_keys[:, 1])
    khi = keys.astype(BF16)
    klo = (keys - khi.astype(F32)).astype(BF16)
    kspec = pl.BlockSpec(keys.shape, lambda i: (0, 0, 0, 0))
    big = pl.BlockSpec((PEER_HEADS, PEER_TOK // LANES, PEER_N_KEYS, LANES), lambda i: (0, i, 0, 0))
    return pl.pallas_call(
        _peer_stats_kernel,
        grid=(ntok // PEER_TOK,),
        in_specs=[pl.BlockSpec((PEER_TOK, PEER_HEADS * LANES), lambda i: (i, 0)), kspec, kspec],
        out_specs=[big] * 3,
        out_shape=[jax.ShapeDtypeStruct((PEER_HEADS, ntok // LANES, PEER_N_KEYS, LANES), F32)] * 3,
        scratch_shapes=[pltpu.VMEM((PEER_N_KEYS, PEER_TOK), F32),
                        pltpu.VMEM((PEER_N_KEYS, PEER_TOK), F32),
                        pltpu.VMEM((2, PEER_TOPK, PEER_TOK), F32),
                        pltpu.VMEM((80, PEER_TOK), F32)],
        compiler_params=_cparams(("parallel",)),
        name="peer_stats",
    )(q, khi, klo)


PEER_TB = 512
PEER_ET = 1024


PEER_TC = 256


def _peer_act_kernel(hnt_ref, u_ref, g_ref):
    act = jnp.dot(u_ref[...], hnt_ref[...], preferred_element_type=F32)
    g_ref[...] = (0.5 * act * (1.0 + lax.erf(act * (1.0 / math.sqrt(2.0))))).astype(BF16)


def _peer_act(hnt, u_all, layer):
    d, ntok = hnt.shape
    nexp = u_all.shape[1]
    return pl.pallas_call(
        _peer_act_kernel,
        grid=(ntok // PEER_TB, nexp // PEER_ET),
        in_specs=[pl.BlockSpec((d, PEER_TB), lambda i, j: (0, i)),
                  pl.BlockSpec((None, PEER_ET, d), lambda i, j: (layer, j, 0))],
        out_specs=pl.BlockSpec((None, PEER_ET, PEER_TB), lambda i, j: (i, j, 0)),
        out_shape=jax.ShapeDtypeStruct((ntok // PEER_TB, nexp, PEER_TB), BF16),
        compiler_params=_cparams(("parallel", "parallel")),
        name="peer_act",
    )(hnt, u_all)


def _peer_mix_kernel(vt_ref, g_ref, tau_ref, e1_ref, e2_ref, x_ref, o_ref, acc_ref, w_ref):
    j = pl.program_id(1)
    a0 = j * (PEER_ET // PEER_N_KEYS)

    @pl.when(j == 0)
    def _():
        acc_ref[...] = jnp.zeros_like(acc_ref)

    for c in range(PEER_TB // PEER_TC):
        cl = slice(c * PEER_TC, (c + 1) * PEER_TC)
        for t in range(c * PEER_TC // LANES, (c + 1) * PEER_TC // LANES):
            tl = slice(t * LANES, (t + 1) * LANES)
            for ai in range(PEER_ET // PEER_N_KEYS):
                gate = jnp.zeros((PEER_N_KEYS, LANES), F32)
                for h in range(PEER_HEADS):
                    tau = tau_ref[h, t, pl.ds(a0 + ai, 1), :]
                    e1 = e1_ref[h, t, pl.ds(a0 + ai, 1), :]
                    e2 = e2_ref[h, t]
                    gate = gate + jnp.where(e2 >= tau, e2, 0.0) * e1
                rows = slice(ai * PEER_N_KEYS, (ai + 1) * PEER_N_KEYS)
                w_ref[rows, tl] = (gate * g_ref[rows, tl].astype(F32)).astype(BF16)
        acc_ref[:, cl] += jnp.dot(vt_ref[...], w_ref[:, cl], preferred_element_type=F32)

    @pl.when(j == pl.num_programs(1) - 1)
    def _():
        o_ref[...] = x_ref[...] + acc_ref[...].T


def _peer_mix(vt_all, layer, g, tau, e1, e2, x):
    ntok, d = x.shape
    nexp = vt_all.shape[1] * PEER_ET
    stat = pl.BlockSpec((PEER_HEADS, PEER_TB // LANES, PEER_N_KEYS, LANES), lambda i, j: (0, i, 0, 0))
    tok = pl.BlockSpec((PEER_TB, d), lambda i, j: (i, 0))
    return pl.pallas_call(
        _peer_mix_kernel,
        grid=(ntok // PEER_TB, nexp // PEER_ET),
        in_specs=[pl.BlockSpec((None, None, d, PEER_ET), lambda i, j: (layer, j, 0, 0)),
                  pl.BlockSpec((None, PEER_ET, PEER_TB), lambda i, j: (i, j, 0)),
                  stat, stat, stat, tok],
        out_specs=tok,
        out_shape=jax.ShapeDtypeStruct((ntok, d), F32),
        scratch_shapes=[pltpu.VMEM((d, PEER_TB), F32), pltpu.VMEM((PEER_ET, PEER_TB), BF16)],
        compiler_params=_cparams(("parallel", "arbitrary")),
        name="peer_mix",
    )(vt_all, g, tau, e1, e2, x)


def _peer_vt(peer_v):
    depth, nexp, d = peer_v.shape
    return peer_v.astype(BF16).reshape(depth, nexp // PEER_ET, PEER_ET, d).transpose(0, 1, 3, 2)


def _peer_ffn(x, norm_gain, w_query, sub_keys, u_all, vt_all, layer):
    q, hnt = _norm_matmul(x, norm_gain, w_query.astype(BF16), emit_xn_t=True)
    tau, e1, e2 = _peer_stats(q, sub_keys)
    g = _peer_act(hnt, u_all, layer)
    return _peer_mix(vt_all, layer, g, tau, e1, e2, x)


def kernel(x, norm_mix, w_in, fourier_w, s5_lam_re, s5_lam_im, s5_log_step, s5_b_re, s5_b_im, s5_c_re, s5_c_im, s5_d, s5_w_glu, attn_q_norm, attn_k_norm, rwkv_mu, rwkv_w0, rwkv_w1, rwkv_w2, rwkv_a0, rwkv_a1, rwkv_a2, rwkv_g1, rwkv_g2, rwkv_k_k, rwkv_k_a, rwkv_r_k, rwkv_ln_w, rwkv_ln_b, branch_norm, w_out, norm_ffn, peer_w_query, peer_sub_keys, peer_u, peer_v):
    bn, seq, d = x.shape
    m = bn * seq
    depth = w_in.shape[0]
    dft_s, dft_c = _dft_tables(seq)
    cc, ss = _rope_tables(seq)
    u_all = peer_u.astype(BF16)
    vt_all = _peer_vt(peer_v)
    xf = x.reshape(m, d)
    for l in range(depth):
        bgain = branch_norm[l].reshape(N_MIXERS, GROUP_WIDTH)
        proj = _norm_matmul(xf, norm_mix[l], w_in[l].astype(BF16), tm=1024)
        proj3 = proj.reshape(bn, seq, PROJ_WIDTH)

        ya = _fnet_mix(proj3, dft_s, dft_c, fourier_w[l], bgain[0]).reshape(m, GROUP_WIDTH)

        b_blk, c_blk, lam1, pw = _s5_params(s5_lam_re[l], s5_lam_im[l], s5_log_step[l],
                                            s5_b_re[l], s5_b_im[l], s5_c_re[l], s5_c_im[l])
        u_s5 = _s5_row_order(proj3[:, :, GROUP_WIDTH:2 * GROUP_WIDTH])
        yf, yb = _s5_scan(u_s5, b_blk, c_blk, lam1, pw)
        ybm = _s5_tail(yf.reshape(m, GROUP_WIDTH), yb.reshape(m, GROUP_WIDTH),
                       u_s5.reshape(m, GROUP_WIDTH), s5_d[l], s5_w_glu[l], bgain[1])
        ybm = _s5_row_order(ybm.reshape(bn, seq, GROUP_WIDTH), inverse=True).reshape(m, GROUP_WIDTH)

        att = _attention_mix(proj3, attn_q_norm[l], attn_k_norm[l], cc, ss)
        yc = _norm_rows(att.reshape(m, GROUP_WIDTH), bgain[2])

        r, v, nkk, g, bonus, w, kd, b = _rwkv_prep(
            proj, seq, rwkv_mu[l], rwkv_w0[l], rwkv_w1[l], rwkv_w2[l], rwkv_a0[l], rwkv_a1[l],
            rwkv_a2[l], rwkv_g1[l], rwkv_g2[l], rwkv_k_k[l], rwkv_k_a[l], rwkv_r_k[l])
        s3 = lambda a: a.reshape(bn, seq, GROUP_WIDTH)
        s4 = lambda a: a.reshape(2, bn, seq, GROUP_WIDTH)
        y_f, y_b = _rwkv_scan(s3(r), s3(v), s3(nkk), s4(w), s4(kd), s4(b))
        yd = _rwkv_post(y_f.reshape(m, GROUP_WIDTH), y_b.reshape(m, GROUP_WIDTH), bonus, g,
                        rwkv_ln_w[l], rwkv_ln_b[l], bgain[3])

        xf = _out_proj(ya, ybm, yc, yd, w_out[l], xf)
        xf = _peer_ffn(xf, norm_ffn[l], peer_w_query[l], peer_sub_keys[l], u_all, vt_all, l)
    return xf.reshape(bn, seq, d)
```

```python
import functools
import math

import jax
import jax.numpy as jnp
import numpy as np
from jax import lax
from jax.experimental import pallas as pl
from jax.experimental.pallas import tpu as pltpu

F32 = jnp.float32
BF16 = jnp.bfloat16

D_MODEL = 2048
GROUP_WIDTH = 512
N_MIXERS = 4
FN_HEADS = 4
FN_HEAD_DIM = 128
S5_GROUP_CH = 16
S5_GROUPS = 32
S5_STATE = 64
S5_NSTATE = S5_GROUPS * S5_STATE
HEAD_DIM = 128
ATT_HEADS = 4
KV_HEADS = 2
GRID_W = 64
ROPE_THETA = 10000.0
AXIS_ROT_DIM = 64
RW_HEAD_DIM = 64
RW_HEADS = 8
RW_GN_EPS = 64e-5
LORA_PAD = 128
PEER_HEADS = 8
PEER_HALF = 64
PEER_N_KEYS = 128
PEER_TOPK = 16
NORM_EPS = 1e-6
PROJ_WIDTH = 4096

LANES = 128
VMEM_LIMIT = 56 * 1024 * 1024


def _cparams(sem):
    return pltpu.CompilerParams(dimension_semantics=sem, vmem_limit_bytes=VMEM_LIMIT)


def _rms(x, gain):
    return x * lax.rsqrt(jnp.mean(x * x, axis=-1, keepdims=True) + NORM_EPS) * gain


def _split2(x):
    hi = x.astype(BF16)
    lo = (x - hi.astype(F32)).astype(BF16)
    return hi, lo


def _norm_matmul_kernel(x_ref, g_ref, w_ref, o_ref, *rest):
    xn_ref = rest[-1]

    @pl.when(pl.program_id(1) == 0)
    def _():
        xn = _rms(x_ref[...], g_ref[...])
        xn_ref[...] = xn.astype(BF16)
        if len(rest) == 2:
            rest[0][...] = xn.T.astype(BF16)

    o_ref[...] = jnp.dot(xn_ref[...], w_ref[...], preferred_element_type=F32)


def _norm_matmul(x, gain, w, *, tm=512, tn=1024, emit_xn_t=False):
    m, k = x.shape
    n = w.shape[1]
    out_specs = [pl.BlockSpec((tm, tn), lambda i, j: (i, j))]
    out_shape = [jax.ShapeDtypeStruct((m, n), F32)]
    if emit_xn_t:
        out_specs.append(pl.BlockSpec((k, tm), lambda i, j: (0, i)))
        out_shape.append(jax.ShapeDtypeStruct((k, m), BF16))
    out = pl.pallas_call(
        _norm_matmul_kernel,
        grid=(m // tm, n // tn),
        in_specs=[
            pl.BlockSpec((tm, k), lambda i, j: (i, 0)),
            pl.BlockSpec((1, k), lambda i, j: (0, 0)),
            pl.BlockSpec((k, tn), lambda i, j: (0, j)),
        ],
        out_specs=out_specs,
        out_shape=out_shape,
        scratch_shapes=[pltpu.VMEM((tm, k), BF16)],
        compiler_params=_cparams(("parallel", "arbitrary")),
        name="norm_matmul",
    )(x, gain.reshape(1, k), w)
    return out if emit_xn_t else out[0]


def _fnet_kernel(pa_ref, dftc_ref, dfts_ref, wf_ref, g_ref, o_ref, z_ref, *, seq):
    @pl.when(pl.program_id(1) == 0)
    def _():
        for h in range(FN_HEADS):
            sl = slice(h * FN_HEAD_DIM, (h + 1) * FN_HEAD_DIM)
            xh = pa_ref[0, :, sl].astype(BF16)
            t = jnp.dot(xh, dftc_ref[...], preferred_element_type=F32)
            z_ref[0:seq, sl] = t[:, :FN_HEAD_DIM].astype(BF16)
            z_ref[seq:2 * seq, sl] = t[:, FN_HEAD_DIM:].astype(BF16)

    scale = 1.0 / math.sqrt(seq * FN_HEAD_DIM)
    re = jnp.dot(dfts_ref[...], z_ref[...], preferred_element_type=F32) * scale
    ya = jnp.dot(re.astype(BF16), wf_ref[...], preferred_element_type=F32)
    o_ref[0] = _rms(ya, g_ref[...]).astype(BF16)


def _dft_tables(seq):
    def cos_sin(n):
        k = np.arange(n, dtype=np.int64)
        ang = ((k[:, None] * k[None, :]) % n).astype(np.float64) * (2.0 * math.pi / n)
        return np.cos(ang), np.sin(ang)

    cs, sn = cos_sin(seq)
    dft_s = jnp.asarray(np.concatenate([cs, -sn], axis=1).astype(np.float32)).astype(BF16)
    cs, sn = cos_sin(FN_HEAD_DIM)
    dft_c = jnp.asarray(np.concatenate([cs, sn], axis=1).astype(np.float32)).astype(BF16)
    return dft_s, dft_c


def _fnet_mix(proj3, dft_s, dft_c, w_f, gain, *, tm=512):
    bn, seq, _ = proj3.shape
    return pl.pallas_call(
        functools.partial(_fnet_kernel, seq=seq),
        grid=(bn, seq // tm),
        in_specs=[
            pl.BlockSpec((1, seq, GROUP_WIDTH), lambda b, r: (b, 0, 0)),
            pl.BlockSpec((FN_HEAD_DIM, 2 * FN_HEAD_DIM), lambda b, r: (0, 0)),
            pl.BlockSpec((tm, 2 * seq), lambda b, r: (r, 0)),
            pl.BlockSpec((GROUP_WIDTH, GROUP_WIDTH), lambda b, r: (0, 0)),
            pl.BlockSpec((1, GROUP_WIDTH), lambda b, r: (0, 0)),
        ],
        out_specs=pl.BlockSpec((1, tm, GROUP_WIDTH), lambda b, r: (b, r, 0)),
        out_shape=jax.ShapeDtypeStruct((bn, seq, GROUP_WIDTH), BF16),
        scratch_shapes=[pltpu.VMEM((2 * seq, GROUP_WIDTH), BF16)],
        compiler_params=_cparams(("parallel", "arbitrary")),
        name="fnet_mix",
    )(proj3, dft_c, dft_s, w_f.astype(BF16), gain.reshape(1, GROUP_WIDTH))


S5_CHUNK = 512
S5_SEGS = 8
S5_STEPS = S5_CHUNK // S5_SEGS
S5_TILES_PER_PASS = 4
S5_HALF_CH = GROUP_WIDTH // 2


def _s5_params(lam_re, lam_im, log_step, b_re, b_im, c_re, c_im):
    step = jnp.exp(log_step.astype(F32))[..., None]
    lr = lam_re.astype(F32)
    li = lam_im.astype(F32)
    ar = lr * step
    ai = li * step
    mag = jnp.exp(ar)
    lbr = mag * jnp.cos(ai)
    lbi = mag * jnp.sin(ai)
    den = lr * lr + li * li
    qr = ((lbr - 1.0) * lr + lbi * li) / den
    qi = (lbi * lr - (lbr - 1.0) * li) / den
    bbr = qr[..., None] * b_re - qi[..., None] * b_im
    bbi = qr[..., None] * b_im + qi[..., None] * b_re
    eye = jnp.eye(S5_GROUPS, dtype=F32)

    def in_blk(b):
        t = jnp.einsum('dgph,gk->dghkp', b, eye)
        return t.reshape(2, GROUP_WIDTH, S5_NSTATE)

    def out_blk(c):
        t = jnp.einsum('dghp,gk->dgpkh', c, eye)
        return t.reshape(2, S5_NSTATE, GROUP_WIDTH)

    hc, hs = S5_HALF_CH, S5_NSTATE // 2
    halves = range(2)
    bre, bim = in_blk(bbr), in_blk(bbi)
    b_blk = jnp.stack([jnp.concatenate([m[:, r * hc:(r + 1) * hc, r * hs:(r + 1) * hs]
                                        for m in (bre, bim)], axis=2) for r in halves],
                      axis=1).astype(BF16)
    cre, cim = out_blk(c_re.astype(F32)), -out_blk(c_im.astype(F32))
    c_blk = jnp.stack([jnp.concatenate([m[:, r * hs:(r + 1) * hs, r * hc:(r + 1) * hc]
                                        for m in (cre, cim)], axis=1) for r in halves],
                      axis=1).astype(BF16)
    kpow = jnp.arange(1, S5_STEPS + 1, dtype=F32)[None, :, None, None]
    pmag = jnp.exp(ar[:, None] * kpow)
    pang = ai[:, None] * kpow
    pw = jnp.concatenate([(pmag * jnp.cos(pang)).reshape(2, S5_STEPS, S5_NSTATE),
                          (pmag * jnp.sin(pang)).reshape(2, S5_STEPS, S5_NSTATE)], axis=2)
    lam1 = pw[:, 0:1, :]
    pw_tiles = pw.reshape(2, S5_STEPS, 2 * S5_NSTATE // LANES, LANES).transpose(0, 2, 1, 3)
    return b_blk, c_blk, lam1, pw_tiles


def _s5_scan_kernel(uf_ref, ub_ref, bblk_ref, cblk_ref, lam_ref, pw_ref, yf_ref, yb_ref,
                    bu_ref, carry_ref, cin_ref):
    ns = S5_NSTATE

    @pl.when(pl.program_id(1) == 0)
    def _():
        carry_ref[...] = jnp.zeros_like(carry_ref)

    nt = ns // LANES
    th = nt // 2
    for d, u_ref in ((0, uf_ref), (1, ub_ref)):
        u = u_ref[0].astype(BF16)
        for rb in range(2):
            bu = jnp.dot(u[:, rb * S5_HALF_CH:(rb + 1) * S5_HALF_CH], bblk_ref[d, rb],
                         preferred_element_type=F32)
            for k in range(th):
                bu_ref[d, rb * th + k] = bu[:, k * LANES:(k + 1) * LANES]
                bu_ref[d, nt + rb * th + k] = bu[:, (th + k) * LANES:(th + k + 1) * LANES]

    def lanes(q):
        return slice(q * LANES, (q + 1) * LANES)

    for q0 in range(0, nt, S5_TILES_PER_PASS):
        tiles = range(q0, q0 + S5_TILES_PER_PASS)
        lam = {(d, q): (lam_ref[d, :, lanes(q)], lam_ref[d, :, lanes(nt + q)])
               for d in range(2) for q in tiles}

        def step(i, hs, tiles=tiles, lam=lam):
            out = []
            for d in range(2):
                row = i if d == 0 else S5_STEPS - 1 - i
                idx = pl.ds(pl.multiple_of(row * S5_SEGS, S5_SEGS), S5_SEGS)
                for q in tiles:
                    hr, hi = hs[len(out)], hs[len(out) + 1]
                    lr, li = lam[d, q]
                    nr = lr * hr - li * hi + bu_ref[d, q, idx, :]
                    ni = lr * hi + li * hr + bu_ref[d, nt + q, idx, :]
                    bu_ref[d, q, idx, :] = nr
                    bu_ref[d, nt + q, idx, :] = ni
                    out += [nr, ni]
            return tuple(out)

        z = jnp.zeros((S5_SEGS, LANES), F32)
        ends = lax.fori_loop(0, S5_STEPS, step, (z,) * (4 * S5_TILES_PER_PASS))

        cin = {}
        pos = 0
        for d in range(2):
            order = range(S5_SEGS) if d == 0 else range(S5_SEGS - 1, -1, -1)
            for q in tiles:
                er, ei = ends[pos], ends[pos + 1]
                pos += 2
                pr = pw_ref[d, q, S5_STEPS - 1:S5_STEPS, :]
                pi = pw_ref[d, nt + q, S5_STEPS - 1:S5_STEPS, :]
                cr = carry_ref[d, :, lanes(q)]
                ci = carry_ref[d, :, lanes(nt + q)]
                for sg in order:
                    cin_ref[d, sg:sg + 1, lanes(q)] = cr
                    cin_ref[d, sg:sg + 1, lanes(nt + q)] = ci
                    nr = er[sg:sg + 1] + pr * cr - pi * ci
                    ni = ei[sg:sg + 1] + pr * ci + pi * cr
                    cr, ci = nr, ni
                carry_ref[d, :, lanes(q)] = cr
                carry_ref[d, :, lanes(nt + q)] = ci
                cin[d, q] = (cin_ref[d, :, lanes(q)], cin_ref[d, :, lanes(nt + q)])

        def fix(i, c, tiles=tiles, cin=cin):
            for d in range(2):
                row = i if d == 0 else S5_STEPS - 1 - i
                idx = pl.ds(pl.multiple_of(row * S5_SEGS, S5_SEGS), S5_SEGS)
                for q in tiles:
                    pr = pw_ref[d, q, pl.ds(i, 1), :]
                    pi = pw_ref[d, nt + q, pl.ds(i, 1), :]
                    cr, ci = cin[d, q]
                    bu_ref[d, q, idx, :] = bu_ref[d, q, idx, :] + pr * cr - pi * ci
                    bu_ref[d, nt + q, idx, :] = bu_ref[d, nt + q, idx, :] + pr * ci + pi * cr
            return c

        lax.fori_loop(0, S5_STEPS, fix, 0)

    for d, y_ref in ((0, yf_ref), (1, yb_ref)):
        ys = []
        for cb in range(2):
            tiles = [cb * th + k for k in range(th)] + [nt + cb * th + k for k in range(th)]
            hs = jnp.concatenate([bu_ref[d, q].astype(BF16) for q in tiles], axis=1)
            ys.append(jnp.dot(hs, cblk_ref[d, cb], preferred_element_type=F32))
        y_ref[0] = jnp.concatenate(ys, axis=1)


def _s5_row_order(x, inverse=False):
    bn, seq, n = x.shape
    a, b = (S5_STEPS, S5_SEGS) if inverse else (S5_SEGS, S5_STEPS)
    return x.reshape(bn, seq // S5_CHUNK, a, b, n).transpose(0, 1, 3, 2, 4).reshape(bn, seq, n)


def _s5_scan(u, b_blk, c_blk, lam1, pw):
    bn, seq, _ = u.shape
    nc = seq // S5_CHUNK
    ns2 = 2 * S5_NSTATE
    u_blk = (1, S5_CHUNK, GROUP_WIDTH)
    return pl.pallas_call(
        _s5_scan_kernel,
        grid=(bn, nc),
        in_specs=[
            pl.BlockSpec(u_blk, lambda b, c: (b, c, 0)),
            pl.BlockSpec(u_blk, lambda b, c: (b, nc - 1 - c, 0)),
            pl.BlockSpec((2, 2, S5_HALF_CH, S5_NSTATE), lambda b, c: (0, 0, 0, 0)),
            pl.BlockSpec((2, 2, S5_NSTATE, S5_HALF_CH), lambda b, c: (0, 0, 0, 0)),
            pl.BlockSpec((2, 1, ns2), lambda b, c: (0, 0, 0)),
            pl.BlockSpec((2, ns2 // LANES, S5_STEPS, LANES), lambda b, c: (0, 0, 0, 0)),
        ],
        out_specs=[
            pl.BlockSpec(u_blk, lambda b, c: (b, c, 0)),
            pl.BlockSpec(u_blk, lambda b, c: (b, nc - 1 - c, 0)),
        ],
        out_shape=[jax.ShapeDtypeStruct((bn, seq, GROUP_WIDTH), F32)] * 2,
        scratch_shapes=[
            pltpu.VMEM((2, ns2 // LANES, S5_CHUNK, LANES), F32),
            pltpu.VMEM((2, 1, ns2), F32),
            pltpu.VMEM((2, S5_SEGS, ns2), F32),
        ],
        compiler_params=_cparams(("parallel", "arbitrary")),
        name="s5_scan",
    )(u, u, b_blk, c_blk, lam1, pw)


def _s5_tail_kernel(yf_ref, yb_ref, u_ref, d_ref, wg_ref, g_ref, o_ref):
    y = yf_ref[...] + yb_ref[...] + d_ref[...] * u_ref[...]
    y = jax.nn.gelu(y)
    gate = jnp.dot(y.astype(BF16), wg_ref[...], preferred_element_type=F32)
    o_ref[...] = _rms(y * jax.nn.sigmoid(gate), g_ref[...]).astype(BF16)


def _s5_tail(yf, yb, u, d_skip, w_glu, gain, *, tm=512):
    m = yf.shape[0]
    row = pl.BlockSpec((tm, GROUP_WIDTH), lambda i: (i, 0))
    vec = pl.BlockSpec((1, GROUP_WIDTH), lambda i: (0, 0))
    return pl.pallas_call(
        _s5_tail_kernel,
        grid=(m // tm,),
        in_specs=[row, row, row, vec,
                  pl.BlockSpec((GROUP_WIDTH, GROUP_WIDTH), lambda i: (0, 0)), vec],
        out_specs=row,
        out_shape=jax.ShapeDtypeStruct((m, GROUP_WIDTH), BF16),
        compiler_params=_cparams(("parallel",)),
        name="s5_tail",
    )(yf, yb, u, d_skip.reshape(1, GROUP_WIDTH), w_glu.astype(BF16), gain.reshape(1, GROUP_WIDTH))


def _rope(x, cc, ss):
    return x * cc + pltpu.roll(x, HEAD_DIM // 2, axis=1) * ss


def _attn_kernel(q_ref, k_ref, v_ref, qg_ref, kg_ref, ccq_ref, ssq_ref, cck_ref, ssk_ref,
                 o_ref, kp_ref, vp_ref):
    @pl.when(pl.program_id(2) == 0)
    def _():
        kn = _rms(k_ref[0], kg_ref[...])
        kp_ref[...] = _rope(kn, cck_ref[...], ssk_ref[...]).astype(BF16)
        vp_ref[...] = v_ref[0].astype(BF16)

    rep = ATT_HEADS // KV_HEADS
    outs = []
    for r in range(rep):
        q = q_ref[0, :, r * HEAD_DIM:(r + 1) * HEAD_DIM]
        qn = _rope(_rms(q, qg_ref[...]), ccq_ref[...], ssq_ref[...]) * (HEAD_DIM ** -0.5)
        s = lax.dot_general(qn.astype(BF16), kp_ref[...], (((1,), (1,)), ((), ())),
                            preferred_element_type=F32)
        p = jnp.exp(s - jnp.max(s, axis=-1, keepdims=True))
        inv = 1.0 / jnp.sum(p, axis=-1, keepdims=True)
        o = jnp.dot(p.astype(BF16), vp_ref[...], preferred_element_type=F32) * inv
        outs.append(o)
    o_ref[0] = jnp.concatenate(outs, axis=1)


def _rope_tables(seq):
    rows = seq // GRID_W
    row_id = np.repeat(np.arange(rows), GRID_W).astype(np.float64)
    col_id = np.tile(np.arange(GRID_W), rows).astype(np.float64)
    inv = ROPE_THETA ** (-np.arange(0, AXIS_ROT_DIM, 2, dtype=np.float64) / AXIS_ROT_DIM)
    ang = np.concatenate([row_id[:, None] * inv, col_id[:, None] * inv], axis=-1)
    cos, sin = np.cos(ang), np.sin(ang)
    cc = np.concatenate([cos, cos], axis=1).astype(np.float32)
    ss = np.concatenate([-sin, sin], axis=1).astype(np.float32)
    return jnp.asarray(cc), jnp.asarray(ss)


def _attention_mix(proj3, q_gain, k_gain, cc, ss, *, tq=512):
    bn, seq, _ = proj3.shape
    rep = ATT_HEADS // KV_HEADS
    qw = rep * HEAD_DIM
    q_col0 = (2 * GROUP_WIDTH) // qw
    k_col0 = (3 * GROUP_WIDTH) // HEAD_DIM
    v_col0 = k_col0 + KV_HEADS
    vec = pl.BlockSpec((1, HEAD_DIM), lambda b, g, i: (0, 0))
    return pl.pallas_call(
        _attn_kernel,
        grid=(bn, KV_HEADS, seq // tq),
        in_specs=[
            pl.BlockSpec((1, tq, qw), lambda b, g, i: (b, i, q_col0 + g)),
            pl.BlockSpec((1, seq, HEAD_DIM), lambda b, g, i: (b, 0, k_col0 + g)),
            pl.BlockSpec((1, seq, HEAD_DIM), lambda b, g, i: (b, 0, v_col0 + g)),
            vec, vec,
            pl.BlockSpec((tq, HEAD_DIM), lambda b, g, i: (i, 0)),
            pl.BlockSpec((tq, HEAD_DIM), lambda b, g, i: (i, 0)),
            pl.BlockSpec((seq, HEAD_DIM), lambda b, g, i: (0, 0)),
            pl.BlockSpec((seq, HEAD_DIM), lambda b, g, i: (0, 0)),
        ],
        out_specs=pl.BlockSpec((1, tq, qw), lambda b, g, i: (b, i, g)),
        out_shape=jax.ShapeDtypeStruct((bn, seq, GROUP_WIDTH), F32),
        scratch_shapes=[pltpu.VMEM((seq, HEAD_DIM), BF16), pltpu.VMEM((seq, HEAD_DIM), BF16)],
        compiler_params=_cparams(("parallel", "parallel", "arbitrary")),
        name="gqa_attention",
    )(proj3, proj3, proj3, q_gain.reshape(1, HEAD_DIM), k_gain.reshape(1, HEAD_DIM),
      cc, ss, cc, ss)


def _head_ones(width):
    r = lax.broadcasted_iota(jnp.int32, (width, width), 0) // RW_HEAD_DIM
    c = lax.broadcasted_iota(jnp.int32, (width, width), 1) // RW_HEAD_DIM
    return (r == c).astype(BF16)


def _head_sum2(x, ones):
    hi, lo = _split2(x)
    return (jnp.dot(hi, ones, preferred_element_type=F32)
            + jnp.dot(lo, ones, preferred_element_type=F32))


def _rwkv_prep_kernel(c_ref, p_ref, n_ref, mu_ref, vecs_ref, dvec_ref,
                      g1_ref, g2_ref, w1_ref, w2_ref, a1_ref, a2_ref,
                      r_o, v_o, nkk_o, g_o, bonus_o, w_o, kd_o, b_o, *, tiles_per_seq):
    i = pl.program_id(0)
    tm = c_ref.shape[0]
    first = (i % tiles_per_seq) == 0
    last = (i % tiles_per_seq) == tiles_per_seq - 1
    rows = lax.broadcasted_iota(jnp.int32, (tm, GROUP_WIDTH), 0)
    ones = _head_ones(GROUP_WIDTH)

    def shifted(j):
        sl = slice(j * GROUP_WIDTH, (j + 1) * GROUP_WIDTH)
        x = c_ref[:, sl]
        prev_row = jnp.where(first, 0.0, p_ref[7:8, sl])
        next_row = jnp.where(last, 0.0, n_ref[0:1, sl])
        prev = jnp.where(rows == 0, prev_row, pltpu.roll(x, 1, axis=0))
        nxt = jnp.where(rows == tm - 1, next_row, pltpu.roll(x, tm - 1, axis=0))
        return x + (prev - x) * mu_ref[j, 0:1, :] + (nxt - x) * mu_ref[j, 1:2, :]

    r, k, v, z = (shifted(j) for j in range(4))
    k_k, k_a, r_k = vecs_ref[0:1, :], vecs_ref[1:2, :], vecs_ref[2:3, :]
    zb = z.astype(BF16)

    gmid = jax.nn.sigmoid(jnp.dot(zb, g1_ref[...], preferred_element_type=F32))
    g_o[...] = jnp.dot(gmid.astype(BF16), g2_ref[...], preferred_element_type=F32)

    kk = k * k_k
    kk = kk * lax.rsqrt(_head_sum2(kk * kk, ones) + 1e-12)
    r_o[...] = r
    v_o[...] = v
    nkk_o[...] = -kk

    bonus = jnp.zeros_like(r)
    for d in range(2):
        w0, a0 = dvec_ref[d, 0:1, :], dvec_ref[d, 1:2, :]
        wmid = jnp.tanh(jnp.dot(zb, w1_ref[d], preferred_element_type=F32))
        wlin = w0 + jnp.dot(wmid.astype(BF16), w2_ref[d], preferred_element_type=F32)
        w_log = -jax.nn.softplus(-wlin) - 0.5
        w_o[d] = -jnp.exp(w_log)
        amid = jnp.dot(zb, a1_ref[d], preferred_element_type=F32)
        a = jax.nn.sigmoid(a0 + jnp.dot(amid.astype(BF16), a2_ref[d], preferred_element_type=F32))
        kd = k * (1.0 + (a - 1.0) * k_a)
        kd_o[d] = kd
        b_o[d] = kk * a
        bonus = bonus + _head_sum2(r * kd * r_k, ones) * v
    bonus_o[...] = bonus


def _pad_cols(w, n):
    return jnp.pad(w, [(0, 0)] * (w.ndim - 1) + [(0, n - w.shape[-1])])


def _pad_rows(w, n):
    return jnp.pad(w, [(0, 0)] * (w.ndim - 2) + [(0, n - w.shape[-2]), (0, 0)])


def _rwkv_prep(proj, seq, mu, w0, w1, w2, a0, a1, a2, g1, g2, k_k, k_a, r_k, *, tm=512):
    m = proj.shape[0]
    wide = 4 * GROUP_WIDTH
    tiles_per_seq = seq // tm
    nblk8 = m // 8
    vecs = jnp.stack([k_k, k_a, r_k.reshape(GROUP_WIDTH)]
                     + [jnp.zeros((GROUP_WIDTH,), F32)] * 5).astype(F32)
    dvec = jnp.stack([jnp.stack([w0[d], a0[d]] + [jnp.zeros((GROUP_WIDTH,), F32)] * 6)
                      for d in range(2)]).astype(F32)
    row = pl.BlockSpec((tm, GROUP_WIDTH), lambda i: (i, 0))
    row2 = pl.BlockSpec((2, tm, GROUP_WIDTH), lambda i: (0, i, 0))
    full = lambda a: pl.BlockSpec(a.shape, lambda i: (0,) * a.ndim)
    g1p = _pad_cols(g1, LORA_PAD).astype(BF16)
    g2p = _pad_rows(g2, LORA_PAD).astype(BF16)
    w1p = _pad_cols(w1, LORA_PAD).astype(BF16)
    w2p = _pad_rows(w2, LORA_PAD).astype(BF16)
    a1p = _pad_cols(a1, LORA_PAD).astype(BF16)
    a2p = _pad_rows(a2, LORA_PAD).astype(BF16)
    sds = jax.ShapeDtypeStruct
    return pl.pallas_call(
        functools.partial(_rwkv_prep_kernel, tiles_per_seq=tiles_per_seq),
        grid=(m // tm,),
        in_specs=[
            pl.BlockSpec((tm, wide), lambda i: (i, 1)),
            pl.BlockSpec((8, wide), lambda i: (jnp.maximum(i * (tm // 8) - 1, 0), 1)),
            pl.BlockSpec((8, wide), lambda i: (jnp.minimum((i + 1) * (tm // 8), nblk8 - 1), 1)),
            full(mu), full(vecs), full(dvec),
            full(g1p), full(g2p), full(w1p), full(w2p), full(a1p), full(a2p),
        ],
        out_specs=[row, row, row, row, row, row2, row2, row2],
        out_shape=[sds((m, GROUP_WIDTH), F32)] * 5 + [sds((2, m, GROUP_WIDTH), F32)] * 3,
        compiler_params=_cparams(("parallel",)),
        name="rwkv_prep",
    )(proj, proj, proj, mu, vecs, dvec, g1p, g2p, w1p, w2p, a1p, a2p)


RW_CHUNK = 64
RW_BATCH_PER_ITER = 4


def _bdot(a, b):
    return jnp.dot(a.astype(BF16), b.astype(BF16), preferred_element_type=F32)


def _bdot_nt(a, b):
    return lax.dot_general(a.astype(BF16), b.astype(BF16), (((1,), (1,)), ((), ())),
                           preferred_element_type=F32)


def _bdot_tn(a, b):
    return lax.dot_general(a.astype(BF16), b.astype(BF16), (((0,), (0,)), ((), ())),
                           preferred_element_type=F32)


def _rwkv_scan_kernel(rf, vf, nf, lf, kf, bf, rb, vb, nb, lb, kb, bb, yf_ref, yb_ref, st_ref,
                      *, bn):
    c = RW_CHUNK
    pw = 2 * RW_HEAD_DIM

    @pl.when(pl.program_id(0) == 0)
    def _():
        st_ref[...] = jnp.zeros_like(st_ref)

    row = lax.broadcasted_iota(jnp.int32, (c, pw), 0)
    col = lax.broadcasted_iota(jnp.int32, (c, pw), 1)
    first = col < RW_HEAD_DIM
    colh = col % RW_HEAD_DIM
    incl2 = {0: (colh <= row).astype(F32), 1: (colh >= row).astype(F32)}
    strict2 = {0: (colh < row).astype(F32), 1: (colh > row).astype(F32)}
    eye2 = (colh == row).astype(F32)
    trow = lax.broadcasted_iota(jnp.int32, (c, c), 0)
    tcol = lax.broadcasted_iota(jnp.int32, (c, c), 1)
    tri = {0: (tcol <= trow).astype(BF16), 1: (tcol >= trow).astype(BF16)}
    br = lax.broadcasted_iota(jnp.int32, (pw, pw), 0) // RW_HEAD_DIM
    bc = lax.broadcasted_iota(jnp.int32, (pw, pw), 1) // RW_HEAD_DIM
    same_head = br == bc

    def bd(x):
        return jnp.concatenate([jnp.where(first, x, 0.0), jnp.where(first, 0.0, x)], axis=0)

    npair = GROUP_WIDTH // pw
    cat = jnp.concatenate

    def body(it, carry):
        ch = []
        chain_inputs = [(it * RW_BATCH_PER_ITER + bi, d, refs)
                        for bi in range(RW_BATCH_PER_ITER)
                        for d, refs in ((0, (rf, vf, nf, lf, kf, bf)), (1, (rb, vb, nb, lb, kb, bb)))]
        for b, d, refs in chain_inputs:
            r_, v_, a_, lw, k_, b_ = (ref[b] for ref in refs)
            hi = lw.astype(BF16)
            r1 = lw - hi.astype(F32)
            mid = r1.astype(BF16)
            lo = (r1 - mid.astype(F32)).astype(BF16)
            cs = (jnp.dot(tri[d], hi, preferred_element_type=F32)
                  + jnp.dot(tri[d], mid, preferred_element_type=F32)
                  + jnp.dot(tri[d], lo, preferred_element_type=F32))
            tot = cs[c - 1:c] if d == 0 else cs[0:1]
            g_incl = jnp.exp(cs)
            inv = jnp.exp(-cs)
            gend = jnp.exp(tot - cs)
            gtot = jnp.exp(tot)
            at = a_ * jnp.exp(cs - lw)
            bt = b_ * inv
            kt = k_ * inv
            rt = r_ * g_incl
            bh = b_ * gend
            kh = k_ * gend
            for p in range(npair):
                sl = slice(p * pw, (p + 1) * pw)
                ch.append(dict(b=b, d=d, idx=(b * 2 + d) * npair + p, vm=v_[:, sl], at=at[:, sl],
                               bt=bt[:, sl], kt=kt[:, sl], rt=rt[:, sl], bh=bh[:, sl],
                               kh=kh[:, sl], gtot=gtot[:, sl]))
        for q in ch:
            o = _bdot_nt(cat([q['at'], q['rt']], axis=0),
                         cat([bd(q['bt']), bd(q['kt'])], axis=0))
            q['a_ab'] = o[:c, :pw] * strict2[q['d']]
            q['a_ak'] = o[:c, pw:] * strict2[q['d']]
            q['q_bk'] = cat([o[c:, :pw] * incl2[q['d']], o[c:, pw:] * incl2[q['d']]], axis=1)
            q['tm'] = eye2 + q['a_ab']
        for q in ch:
            q['pp'] = _bdot(q['a_ab'], bd(q['a_ab']))
        for i in range(5):
            for q in ch:
                rr = _bdot(cat([q['tm'], q['pp']], axis=0), bd(q['pp']))
                q['tm'] = q['tm'] + rr[:c]
                q['pp'] = rr[c:]
        for q in ch:
            q['akv'] = _bdot(q['a_ak'], bd(q['vm']))
        for q in ch:
            q['w_uv'] = _bdot(q['tm'], cat([bd(q['at']), bd(q['akv'])], axis=1))
        for q in ch:
            q['s0'] = st_ref[q['idx']]
            q['uy'] = _bdot_nt(cat([q['w_uv'][:, :pw], q['rt']], axis=0), q['s0'])
        for q in ch:
            q['u'] = q['uy'][:c] + q['w_uv'][:, pw:]
            upd = _bdot_tn(cat([q['u'], q['vm']], axis=0), cat([q['bh'], q['kh']], axis=0))
            st_ref[q['idx']] = q['s0'] * q['gtot'] + jnp.where(same_head, upd, 0.0)
        for q in ch:
            q['y'] = q['uy'][c:] + _bdot(q['q_bk'], cat([bd(q['u']), bd(q['vm'])], axis=0))
        for k, (b, d, _) in enumerate(chain_inputs):
            y_ref = yf_ref if d == 0 else yb_ref
            y_ref[b] = cat([q['y'] for q in ch[k * npair:(k + 1) * npair]], axis=1)
        return carry

    lax.fori_loop(0, bn // RW_BATCH_PER_ITER, body, 0)


def _rwkv_scan(r, v, nkk, w, kd, b):
    bn, seq, _ = r.shape
    nc = seq // RW_CHUNK
    blk = (bn, RW_CHUNK, GROUP_WIDTH)
    fwd = pl.BlockSpec(blk, lambda c: (0, c, 0))
    bwd = pl.BlockSpec(blk, lambda c: (0, nc - 1 - c, 0))
    fwd_d = pl.BlockSpec((None,) + blk, lambda c: (0, 0, c, 0))
    bwd_d = pl.BlockSpec((None,) + blk, lambda c: (1, 0, nc - 1 - c, 0))
    return pl.pallas_call(
        functools.partial(_rwkv_scan_kernel, bn=bn),
        grid=(nc,),
        in_specs=[fwd, fwd, fwd, fwd_d, fwd_d, fwd_d, bwd, bwd, bwd, bwd_d, bwd_d, bwd_d],
        out_specs=[fwd, bwd],
        out_shape=[jax.ShapeDtypeStruct((bn, seq, GROUP_WIDTH), F32)] * 2,
        scratch_shapes=[pltpu.VMEM((2 * bn * (RW_HEADS // 2), 2 * RW_HEAD_DIM, 2 * RW_HEAD_DIM), F32)],
        compiler_params=_cparams(("arbitrary",)),
        name="rwkv_scan",
    )(r, v, nkk, w, kd, b, r, v, nkk, w, kd, b)


def _rwkv_post_kernel(yf_ref, yb_ref, bonus_ref, g_ref, lnw_ref, lnb_ref, gain_ref, o_ref):
    ones = _head_ones(GROUP_WIDTH)
    y = yf_ref[...] + yb_ref[...]
    mean = _head_sum2(y, ones) * (1.0 / RW_HEAD_DIM)
    yc = y - mean
    var = _head_sum2(yc * yc, ones) * (1.0 / RW_HEAD_DIM)
    yn = yc * lax.rsqrt(var + RW_GN_EPS) * lnw_ref[...] + lnb_ref[...]
    out = (yn + bonus_ref[...]) * g_ref[...]
    o_ref[...] = _rms(out, gain_ref[...]).astype(BF16)


def _rwkv_post(yf, yb, bonus, g, ln_w, ln_b, gain, *, tm=512):
    m = yf.shape[0]
    row = pl.BlockSpec((tm, GROUP_WIDTH), lambda i: (i, 0))
    vec = pl.BlockSpec((1, GROUP_WIDTH), lambda i: (0, 0))
    v2 = lambda a: a.reshape(1, GROUP_WIDTH)
    return pl.pallas_call(
        _rwkv_post_kernel,
        grid=(m // tm,),
        in_specs=[row, row, row, row, vec, vec, vec],
        out_specs=row,
        out_shape=jax.ShapeDtypeStruct((m, GROUP_WIDTH), BF16),
        compiler_params=_cparams(("parallel",)),
        name="rwkv_post",
    )(yf, yb, bonus, g, v2(ln_w), v2(ln_b), v2(gain))


def _norm_rows_kernel(x_ref, g_ref, o_ref):
    o_ref[...] = _rms(x_ref[...], g_ref[...]).astype(BF16)


def _norm_rows(x, gain, *, tm=512):
    m, n = x.shape
    return pl.pallas_call(
        _norm_rows_kernel,
        grid=(m // tm,),
        in_specs=[pl.BlockSpec((tm, n), lambda i: (i, 0)), pl.BlockSpec((1, n), lambda i: (0, 0))],
        out_specs=pl.BlockSpec((tm, n), lambda i: (i, 0)),
        out_shape=jax.ShapeDtypeStruct((m, n), BF16),
        compiler_params=_cparams(("parallel",)),
        name="norm_rows",
    )(x, gain.reshape(1, n))


def _out_proj_kernel(ya_ref, yb_ref, yc_ref, yd_ref, w_ref, x_ref, o_ref):
    acc = x_ref[...]
    for g, y_ref in enumerate((ya_ref, yb_ref, yc_ref, yd_ref)):
        acc = acc + jnp.dot(y_ref[...], w_ref[g * GROUP_WIDTH:(g + 1) * GROUP_WIDTH, :],
                            preferred_element_type=F32)
    o_ref[...] = acc


def _out_proj(ya, yb, yc, yd, w_out, x, *, tm=1024, tn=1024):
    m, n = x.shape
    row = pl.BlockSpec((tm, GROUP_WIDTH), lambda i, j: (i, 0))
    return pl.pallas_call(
        _out_proj_kernel,
        grid=(m // tm, n // tn),
        in_specs=[row, row, row, row,
                  pl.BlockSpec((N_MIXERS * GROUP_WIDTH, tn), lambda i, j: (0, j)),
                  pl.BlockSpec((tm, tn), lambda i, j: (i, j))],
        out_specs=pl.BlockSpec((tm, tn), lambda i, j: (i, j)),
        out_shape=jax.ShapeDtypeStruct((m, n), F32),
        compiler_params=_cparams(("parallel", "parallel")),
        name="out_proj",
    )(ya, yb, yc, yd, w_out.astype(BF16), x)


PEER_TOK = 256
NEG_INF = float("-inf")


def _peer_stats_kernel(q_ref, khi_ref, klo_ref, tau_o, e1_o, e2_o,
                       sc1_ref, sc2_ref, top_ref, cand_ref):
    nt = q_ref.shape[0]
    for h in range(PEER_HEADS):
        qh = q_ref[:, h * LANES:(h + 1) * LANES]
        qhi, qlo = _split2(qh)
        nt_dot = lambda a, b: lax.dot_general(a, b, (((1,), (1,)), ((), ())),
                                              preferred_element_type=F32)
        for p in range(2):
            sc = (nt_dot(khi_ref[h, p], qhi) + nt_dot(khi_ref[h, p], qlo)
                  + nt_dot(klo_ref[h, p], qhi))
            if p == 0:
                sc1_ref[...] = sc
            else:
                sc2_ref[...] = sc
            x = sc
            for i in range(PEER_TOPK):
                m = jnp.max(x, axis=0, keepdims=True)
                top_ref[p, i:i + 1, :] = m
                x = jnp.where(x == m, NEG_INF, x)
        s2 = top_ref[1]
        sub = lax.broadcasted_iota(jnp.int32, (8, nt), 0)
        cand_ref[0:16, :] = top_ref[0, 0:1, :] + s2
        cand_ref[16:24, :] = top_ref[0, 1:2, :] + s2[0:8]
        for g, (i, nj) in enumerate(((2, 5), (3, 4), (4, 3), (5, 2), (6, 2), (7, 2))):
            cand_ref[24 + 8 * g:32 + 8 * g, :] = jnp.where(sub < nj, top_ref[0, i:i + 1, :] + s2[0:8],
                                                          NEG_INF)
        cand_ref[72:80, :] = top_ref[0, 8:16, :] + s2[0:1]
        x = cand_ref[...]
        top = top_ref[0, 0:1, :] + top_ref[1, 0:1, :]
        zsum = jnp.zeros((1, nt), F32)
        m = top
        for i in range(PEER_TOPK):
            m = jnp.max(x, axis=0, keepdims=True)
            zsum = zsum + jnp.exp(m - top)
            x = jnp.where(x == m, NEG_INF, x)
        thr = m
        sc1 = sc1_ref[...]
        th = jnp.full(sc1.shape, jnp.inf, F32)
        for i in range(PEER_TOPK):
            s1i = top_ref[0, i:i + 1, :]
            row_th = jnp.min(jnp.where(s1i + s2 >= thr, s2, jnp.inf), axis=0, keepdims=True)
            th = jnp.where(sc1 == s1i, row_th, th)
        e1 = jnp.exp(sc1 - top_ref[0, 0:1, :]) / zsum
        e2 = jnp.exp(sc2_ref[...] - top_ref[1, 0:1, :])
        tau = jnp.exp(th - top_ref[1, 0:1, :])
        for t in range(nt // LANES):
            tl = slice(t * LANES, (t + 1) * LANES)
            tau_o[h, t] = tau[:, tl]
            e1_o[h, t] = e1[:, tl]
            e2_o[h, t] = e2[:, tl]


def _peer_stats(q, sub_keys):
    ntok = q.shape[0]
    keys = jnp.zeros((PEER_HEADS, 2, PEER_N_KEYS, LANES), F32)
    keys = keys.at[:, 0, :, :PEER_HALF].set(sub_keys[:, 0]).at[:, 1, :, PEER_HALF:].set(sub_keys[:, 1])
    khi = keys.astype(BF16)
    klo = (keys - khi.astype(F32)).astype(BF16)
    kspec = pl.BlockSpec(keys.shape, lambda i: (0, 0, 0, 0))
    big = pl.BlockSpec((PEER_HEADS, PEER_TOK // LANES, PEER_N_KEYS, LANES), lambda i: (0, i, 0, 0))
    return pl.pallas_call(
        _peer_stats_kernel,
        grid=(ntok // PEER_TOK,),
        in_specs=[pl.BlockSpec((PEER_TOK, PEER_HEADS * LANES), lambda i: (i, 0)), kspec, kspec],
        out_specs=[big] * 3,
        out_shape=[jax.ShapeDtypeStruct((PEER_HEADS, ntok // LANES, PEER_N_KEYS, LANES), F32)] * 3,
        scratch_shapes=[pltpu.VMEM((PEER_N_KEYS, PEER_TOK), F32),
                        pltpu.VMEM((PEER_N_KEYS, PEER_TOK), F32),
                        pltpu.VMEM((2, PEER_TOPK, PEER_TOK), F32),
                        pltpu.VMEM((80, PEER_TOK), F32)],
        compiler_params=_cparams(("parallel",)),
        name="peer_stats",
    )(q, khi, klo)


PEER_TB = 512
PEER_ET = 1024


PEER_TC = 256


def _peer_act_kernel(hnt_ref, u_ref, g_ref):
    act = jnp.dot(u_ref[...], hnt_ref[...], preferred_element_type=F32)
    g_ref[...] = (0.5 * act * (1.0 + lax.erf(act * (1.0 / math.sqrt(2.0))))).astype(BF16)


def _peer_act(hnt, u_all, layer):
    d, ntok = hnt.shape
    nexp = u_all.shape[1]
    return pl.pallas_call(
        _peer_act_kernel,
        grid=(ntok // PEER_TB, nexp // PEER_ET),
        in_specs=[pl.BlockSpec((d, PEER_TB), lambda i, j: (0, i)),
                  pl.BlockSpec((None, PEER_ET, d), lambda i, j: (layer, j, 0))],
        out_specs=pl.BlockSpec((None, PEER_ET, PEER_TB), lambda i, j: (i, j, 0)),
        out_shape=jax.ShapeDtypeStruct((ntok // PEER_TB, nexp, PEER_TB), BF16),
        compiler_params=_cparams(("parallel", "parallel")),
        name="peer_act",
    )(hnt, u_all)


def _peer_mix_kernel(vt_ref, g_ref, tau_ref, e1_ref, e2_ref, x_ref, o_ref, acc_ref, w_ref):
    j = pl.program_id(1)
    a0 = j * (PEER_ET // PEER_N_KEYS)

    @pl.when(j == 0)
    def _():
        acc_ref[...] = jnp.zeros_like(acc_ref)

    for c in range(PEER_TB // PEER_TC):
        cl = slice(c * PEER_TC, (c + 1) * PEER_TC)
        for t in range(c * PEER_TC // LANES, (c + 1) * PEER_TC // LANES):
            tl = slice(t * LANES, (t + 1) * LANES)
            for ai in range(PEER_ET // PEER_N_KEYS):
                gate = jnp.zeros((PEER_N_KEYS, LANES), F32)
                for h in range(PEER_HEADS):
                    tau = tau_ref[h, t, pl.ds(a0 + ai, 1), :]
                    e1 = e1_ref[h, t, pl.ds(a0 + ai, 1), :]
                    e2 = e2_ref[h, t]
                    gate = gate + jnp.where(e2 >= tau, e2, 0.0) * e1
                rows = slice(ai * PEER_N_KEYS, (ai + 1) * PEER_N_KEYS)
                w_ref[rows, tl] = (gate * g_ref[rows, tl].astype(F32)).astype(BF16)
        acc_ref[:, cl] += jnp.dot(vt_ref[...], w_ref[:, cl], preferred_element_type=F32)

    @pl.when(j == pl.num_programs(1) - 1)
    def _():
        o_ref[...] = x_ref[...] + acc_ref[...].T


def _peer_mix(vt_all, layer, g, tau, e1, e2, x):
    ntok, d = x.shape
    nexp = vt_all.shape[1] * PEER_ET
    stat = pl.BlockSpec((PEER_HEADS, PEER_TB // LANES, PEER_N_KEYS, LANES), lambda i, j: (0, i, 0, 0))
    tok = pl.BlockSpec((PEER_TB, d), lambda i, j: (i, 0))
    return pl.pallas_call(
        _peer_mix_kernel,
        grid=(ntok // PEER_TB, nexp // PEER_ET),
        in_specs=[pl.BlockSpec((None, None, d, PEER_ET), lambda i, j: (layer, j, 0, 0)),
                  pl.BlockSpec((None, PEER_ET, PEER_TB), lambda i, j: (i, j, 0)),
                  stat, stat, stat, tok],
        out_specs=tok,
        out_shape=jax.ShapeDtypeStruct((ntok, d), F32),
        scratch_shapes=[pltpu.VMEM((d, PEER_TB), F32), pltpu.VMEM((PEER_ET, PEER_TB), BF16)],
        compiler_params=_cparams(("parallel", "arbitrary")),
        name="peer_mix",
    )(vt_all, g, tau, e1, e2, x)


def _peer_vt(peer_v):
    depth, nexp, d = peer_v.shape
    return peer_v.astype(BF16).reshape(depth, nexp // PEER_ET, PEER_ET, d).transpose(0, 1, 3, 2)


def _peer_ffn(x, norm_gain, w_query, sub_keys, u_all, vt_all, layer):
    q, hnt = _norm_matmul(x, norm_gain, w_query.astype(BF16), emit_xn_t=True)
    tau, e1, e2 = _peer_stats(q, sub_keys)
    g = _peer_act(hnt, u_all, layer)
    return _peer_mix(vt_all, layer, g, tau, e1, e2, x)


def kernel(x, norm_mix, w_in, fourier_w, s5_lam_re, s5_lam_im, s5_log_step, s5_b_re, s5_b_im, s5_c_re, s5_c_im, s5_d, s5_w_glu, attn_q_norm, attn_k_norm, rwkv_mu, rwkv_w0, rwkv_w1, rwkv_w2, rwkv_a0, rwkv_a1, rwkv_a2, rwkv_g1, rwkv_g2, rwkv_k_k, rwkv_k_a, rwkv_r_k, rwkv_ln_w, rwkv_ln_b, branch_norm, w_out, norm_ffn, peer_w_query, peer_sub_keys, peer_u, peer_v):
    bn, seq, d = x.shape
    m = bn * seq
    depth = w_in.shape[0]
    dft_s, dft_c = _dft_tables(seq)
    cc, ss = _rope_tables(seq)
    u_all = peer_u.astype(BF16)
    vt_all = _peer_vt(peer_v)
    xf = x.reshape(m, d)
    for l in range(depth):
        bgain = branch_norm[l].reshape(N_MIXERS, GROUP_WIDTH)
        proj = _norm_matmul(xf, norm_mix[l], w_in[l].astype(BF16), tm=1024)
        proj3 = proj.reshape(bn, seq, PROJ_WIDTH)

        ya = _fnet_mix(proj3, dft_s, dft_c, fourier_w[l], bgain[0]).reshape(m, GROUP_WIDTH)

        b_blk, c_blk, lam1, pw = _s5_params(s5_lam_re[l], s5_lam_im[l], s5_log_step[l],
                                            s5_b_re[l], s5_b_im[l], s5_c_re[l], s5_c_im[l])
        u_s5 = _s5_row_order(proj3[:, :, GROUP_WIDTH:2 * GROUP_WIDTH])
        yf, yb = _s5_scan(u_s5, b_blk, c_blk, lam1, pw)
        ybm = _s5_tail(yf.reshape(m, GROUP_WIDTH), yb.reshape(m, GROUP_WIDTH),
                       u_s5.reshape(m, GROUP_WIDTH), s5_d[l], s5_w_glu[l], bgain[1])
        ybm = _s5_row_order(ybm.reshape(bn, seq, GROUP_WIDTH), inverse=True).reshape(m, GROUP_WIDTH)

        att = _attention_mix(proj3, attn_q_norm[l], attn_k_norm[l], cc, ss)
        yc = _norm_rows(att.reshape(m, GROUP_WIDTH), bgain[2])

        r, v, nkk, g, bonus, w, kd, b = _rwkv_prep(
            proj, seq, rwkv_mu[l], rwkv_w0[l], rwkv_w1[l], rwkv_w2[l], rwkv_a0[l], rwkv_a1[l],
            rwkv_a2[l], rwkv_g1[l], rwkv_g2[l], rwkv_k_k[l], rwkv_k_a[l], rwkv_r_k[l])
        s3 = lambda a: a.reshape(bn, seq, GROUP_WIDTH)
        s4 = lambda a: a.reshape(2, bn, seq, GROUP_WIDTH)
        y_f, y_b = _rwkv_scan(s3(r), s3(v), s3(nkk), s4(w), s4(kd), s4(b))
        yd = _rwkv_post(y_f.reshape(m, GROUP_WIDTH), y_b.reshape(m, GROUP_WIDTH), bonus, g,
                        rwkv_ln_w[l], rwkv_ln_b[l], bgain[3])

        xf = _out_proj(ya, ybm, yc, yd, w_out[l], xf)
        xf = _peer_ffn(xf, norm_ffn[l], peer_w_query[l], peer_sub_keys[l], u_all, vt_all, l)
    return xf.reshape(bn, seq, d)
```

```python
import functools
import math

import jax
import jax.numpy as jnp
import numpy as np
from jax import lax
from jax.experimental import pallas as pl
from jax.experimental.pallas import tpu as pltpu

F32 = jnp.float32
BF16 = jnp.bfloat16

D_MODEL = 2048
GROUP_WIDTH = 512
N_MIXERS = 4
FN_HEADS = 4
FN_HEAD_DIM = 128
S5_GROUP_CH = 16
S5_GROUPS = 32
S5_STATE = 64
S5_NSTATE = S5_GROUPS * S5_STATE
HEAD_DIM = 128
ATT_HEADS = 4
KV_HEADS = 2
GRID_W = 64
ROPE_THETA = 10000.0
AXIS_ROT_DIM = 64
RW_HEAD_DIM = 64
RW_HEADS = 8
RW_GN_EPS = 64e-5
LORA_PAD = 128
PEER_HEADS = 8
PEER_HALF = 64
PEER_N_KEYS = 128
PEER_TOPK = 16
NORM_EPS = 1e-6
PROJ_WIDTH = 4096

LANES = 128
VMEM_LIMIT = 56 * 1024 * 1024


def _cparams(sem):
    return pltpu.CompilerParams(dimension_semantics=sem, vmem_limit_bytes=VMEM_LIMIT)


def _rms(x, gain):
    return x * lax.rsqrt(jnp.mean(x * x, axis=-1, keepdims=True) + NORM_EPS) * gain


def _split2(x):
    hi = x.astype(BF16)
    lo = (x - hi.astype(F32)).astype(BF16)
    return hi, lo


def _norm_matmul_kernel(x_ref, g_ref, w_ref, o_ref, *rest):
    xn_ref = rest[-1]

    @pl.when(pl.program_id(1) == 0)
    def _():
        xn = _rms(x_ref[...], g_ref[...])
        xn_ref[...] = xn.astype(BF16)
        if len(rest) == 2:
            rest[0][...] = xn.T.astype(BF16)

    o_ref[...] = jnp.dot(xn_ref[...], w_ref[...], preferred_element_type=F32)


def _norm_matmul(x, gain, w, *, tm=512, tn=1024, emit_xn_t=False):
    m, k = x.shape
    n = w.shape[1]
    out_specs = [pl.BlockSpec((tm, tn), lambda i, j: (i, j))]
    out_shape = [jax.ShapeDtypeStruct((m, n), F32)]
    if emit_xn_t:
        out_specs.append(pl.BlockSpec((k, tm), lambda i, j: (0, i)))
        out_shape.append(jax.ShapeDtypeStruct((k, m), BF16))
    out = pl.pallas_call(
        _norm_matmul_kernel,
        grid=(m // tm, n // tn),
        in_specs=[
            pl.BlockSpec((tm, k), lambda i, j: (i, 0)),
            pl.BlockSpec((1, k), lambda i, j: (0, 0)),
            pl.BlockSpec((k, tn), lambda i, j: (0, j)),
        ],
        out_specs=out_specs,
        out_shape=out_shape,
        scratch_shapes=[pltpu.VMEM((tm, k), BF16)],
        compiler_params=_cparams(("parallel", "arbitrary")),
        name="norm_matmul",
    )(x, gain.reshape(1, k), w)
    return out if emit_xn_t else out[0]


def _fnet_kernel(pa_ref, dftc_ref, dfts_ref, wf_ref, g_ref, o_ref, z_ref, *, seq):
    @pl.when(pl.program_id(1) == 0)
    def _():
        for h in range(FN_HEADS):
            sl = slice(h * FN_HEAD_DIM, (h + 1) * FN_HEAD_DIM)
            xh = pa_ref[0, :, sl].astype(BF16)
            t = jnp.dot(xh, dftc_ref[...], preferred_element_type=F32)
            z_ref[0:seq, sl] = t[:, :FN_HEAD_DIM].astype(BF16)
            z_ref[seq:2 * seq, sl] = t[:, FN_HEAD_DIM:].astype(BF16)

    scale = 1.0 / math.sqrt(seq * FN_HEAD_DIM)
    re = jnp.dot(dfts_ref[...], z_ref[...], preferred_element_type=F32) * scale
    ya = jnp.dot(re.astype(BF16), wf_ref[...], preferred_element_type=F32)
    o_ref[0] = _rms(ya, g_ref[...]).astype(BF16)


def _dft_tables(seq):
    def cos_sin(n):
        k = np.arange(n, dtype=np.int64)
        ang = ((k[:, None] * k[None, :]) % n).astype(np.float64) * (2.0 * math.pi / n)
        return np.cos(ang), np.sin(ang)

    cs, sn = cos_sin(seq)
    dft_s = jnp.asarray(np.concatenate([cs, -sn], axis=1).astype(np.float32)).astype(BF16)
    cs, sn = cos_sin(FN_HEAD_DIM)
    dft_c = jnp.asarray(np.concatenate([cs, sn], axis=1).astype(np.float32)).astype(BF16)
    return dft_s, dft_c


def _fnet_mix(proj3, dft_s, dft_c, w_f, gain, *, tm=512):
    bn, seq, _ = proj3.shape
    return pl.pallas_call(
        functools.partial(_fnet_kernel, seq=seq),
        grid=(bn, seq // tm),
        in_specs=[
            pl.BlockSpec((1, seq, GROUP_WIDTH), lambda b, r: (b, 0, 0)),
            pl.BlockSpec((FN_HEAD_DIM, 2 * FN_HEAD_DIM), lambda b, r: (0, 0)),
            pl.BlockSpec((tm, 2 * seq), lambda b, r: (r, 0)),
            pl.BlockSpec((GROUP_WIDTH, GROUP_WIDTH), lambda b, r: (0, 0)),
            pl.BlockSpec((1, GROUP_WIDTH), lambda b, r: (0, 0)),
        ],
        out_specs=pl.BlockSpec((1, tm, GROUP_WIDTH), lambda b, r: (b, r, 0)),
        out_shape=jax.ShapeDtypeStruct((bn, seq, GROUP_WIDTH), BF16),
        scratch_shapes=[pltpu.VMEM((2 * seq, GROUP_WIDTH), BF16)],
        compiler_params=_cparams(("parallel", "arbitrary")),
        name="fnet_mix",
    )(proj3, dft_c, dft_s, w_f.astype(BF16), gain.reshape(1, GROUP_WIDTH))


S5_CHUNK = 512
S5_SEGS = 8
S5_STEPS = S5_CHUNK // S5_SEGS
S5_TILES_PER_PASS = 4
S5_HALF_CH = GROUP_WIDTH // 2


def _s5_params(lam_re, lam_im, log_step, b_re, b_im, c_re, c_im):
    step = jnp.exp(log_step.astype(F32))[..., None]
    lr = lam_re.astype(F32)
    li = lam_im.astype(F32)
    ar = lr * step
    ai = li * step
    mag = jnp.exp(ar)
    lbr = mag * jnp.cos(ai)
    lbi = mag * jnp.sin(ai)
    den = lr * lr + li * li
    qr = ((lbr - 1.0) * lr + lbi * li) / den
    qi = (lbi * lr - (lbr - 1.0) * li) / den
    bbr = qr[..., None] * b_re - qi[..., None] * b_im
    bbi = qr[..., None] * b_im + qi[..., None] * b_re
    eye = jnp.eye(S5_GROUPS, dtype=F32)

    def in_blk(b):
        t = jnp.einsum('dgph,gk->dghkp', b, eye)
        return t.reshape(2, GROUP_WIDTH, S5_NSTATE)

    def out_blk(c):
        t = jnp.einsum('dghp,gk->dgpkh', c, eye)
        return t.reshape(2, S5_NSTATE, GROUP_WIDTH)

    hc, hs = S5_HALF_CH, S5_NSTATE // 2
    halves = range(2)
    bre, bim = in_blk(bbr), in_blk(bbi)
    b_blk = jnp.stack([jnp.concatenate([m[:, r * hc:(r + 1) * hc, r * hs:(r + 1) * hs]
                                        for m in (bre, bim)], axis=2) for r in halves],
                      axis=1).astype(BF16)
    cre, cim = out_blk(c_re.astype(F32)), -out_blk(c_im.astype(F32))
    c_blk = jnp.stack([jnp.concatenate([m[:, r * hs:(r + 1) * hs, r * hc:(r + 1) * hc]
                                        for m in (cre, cim)], axis=1) for r in halves],
                      axis=1).astype(BF16)
    kpow = jnp.arange(1, S5_STEPS + 1, dtype=F32)[None, :, None, None]
    pmag = jnp.exp(ar[:, None] * kpow)
    pang = ai[:, None] * kpow
    pw = jnp.concatenate([(pmag * jnp.cos(pang)).reshape(2, S5_STEPS, S5_NSTATE),
                          (pmag * jnp.sin(pang)).reshape(2, S5_STEPS, S5_NSTATE)], axis=2)
    lam1 = pw[:, 0:1, :]
    pw_tiles = pw.reshape(2, S5_STEPS, 2 * S5_NSTATE // LANES, LANES).transpose(0, 2, 1, 3)
    return b_blk, c_blk, lam1, pw_tiles


def _s5_scan_kernel(uf_ref, ub_ref, bblk_ref, cblk_ref, lam_ref, pw_ref, yf_ref, yb_ref,
                    bu_ref, carry_ref, cin_ref):
    ns = S5_NSTATE

    @pl.when(pl.program_id(1) == 0)
    def _():
        carry_ref[...] = jnp.zeros_like(carry_ref)

    nt = ns // LANES
    th = nt // 2
    for d, u_ref in ((0, uf_ref), (1, ub_ref)):
        u = u_ref[0].astype(BF16)
        for rb in range(2):
            bu = jnp.dot(u[:, rb * S5_HALF_CH:(rb + 1) * S5_HALF_CH], bblk_ref[d, rb],
                         preferred_element_type=F32)
            for k in range(th):
                bu_ref[d, rb * th + k] = bu[:, k * LANES:(k + 1) * LANES]
                bu_ref[d, nt + rb * th + k] = bu[:, (th + k) * LANES:(th + k + 1) * LANES]

    def lanes(q):
        return slice(q * LANES, (q + 1) * LANES)

    for q0 in range(0, nt, S5_TILES_PER_PASS):
        tiles = range(q0, q0 + S5_TILES_PER_PASS)
        lam = {(d, q): (lam_ref[d, :, lanes(q)], lam_ref[d, :, lanes(nt + q)])
               for d in range(2) for q in tiles}

        def step(i, hs, tiles=tiles, lam=lam):
            out = []
            for d in range(2):
                row = i if d == 0 else S5_STEPS - 1 - i
                idx = pl.ds(pl.multiple_of(row * S5_SEGS, S5_SEGS), S5_SEGS)
                for q in tiles:
                    hr, hi = hs[len(out)], hs[len(out) + 1]
                    lr, li = lam[d, q]
                    nr = lr * hr - li * hi + bu_ref[d, q, idx, :]
                    ni = lr * hi + li * hr + bu_ref[d, nt + q, idx, :]
                    bu_ref[d, q, idx, :] = nr
                    bu_ref[d, nt + q, idx, :] = ni
                    out += [nr, ni]
            return tuple(out)

        z = jnp.zeros((S5_SEGS, LANES), F32)
        ends = lax.fori_loop(0, S5_STEPS, step, (z,) * (4 * S5_TILES_PER_PASS))

        cin = {}
        pos = 0
        for d in range(2):
            order = range(S5_SEGS) if d == 0 else range(S5_SEGS - 1, -1, -1)
            for q in tiles:
                er, ei = ends[pos], ends[pos + 1]
                pos += 2
                pr = pw_ref[d, q, S5_STEPS - 1:S5_STEPS, :]
                pi = pw_ref[d, nt + q, S5_STEPS - 1:S5_STEPS, :]
                cr = carry_ref[d, :, lanes(q)]
                ci = carry_ref[d, :, lanes(nt + q)]
                for sg in order:
                    cin_ref[d, sg:sg + 1, lanes(q)] = cr
                    cin_ref[d, sg:sg + 1, lanes(nt + q)] = ci
                    nr = er[sg:sg + 1] + pr * cr - pi * ci
                    ni = ei[sg:sg + 1] + pr * ci + pi * cr
                    cr, ci = nr, ni
                carry_ref[d, :, lanes(q)] = cr
                carry_ref[d, :, lanes(nt + q)] = ci
                cin[d, q] = (cin_ref[d, :, lanes(q)], cin_ref[d, :, lanes(nt + q)])

        def fix(i, c, tiles=tiles, cin=cin):
            for d in range(2):
                row = i if d == 0 else S5_STEPS - 1 - i
                idx = pl.ds(pl.multiple_of(row * S5_SEGS, S5_SEGS), S5_SEGS)
                for q in tiles:
                    pr = pw_ref[d, q, pl.ds(i, 1), :]
                    pi = pw_ref[d, nt + q, pl.ds(i, 1), :]
                    cr, ci = cin[d, q]
                    bu_ref[d, q, idx, :] = bu_ref[d, q, idx, :] + pr * cr - pi * ci
                    bu_ref[d, nt + q, idx, :] = bu_ref[d, nt + q, idx, :] + pr * ci + pi * cr
            return c

        lax.fori_loop(0, S5_STEPS, fix, 0)

    for d, y_ref in ((0, yf_ref), (1, yb_ref)):
        ys = []
        for cb in range(2):
            tiles = [cb * th + k for k in range(th)] + [nt + cb * th + k for k in range(th)]
            hs = jnp.concatenate([bu_ref[d, q].astype(BF16) for q in tiles], axis=1)
            ys.append(jnp.dot(hs, cblk_ref[d, cb], preferred_element_type=F32))
        y_ref[0] = jnp.concatenate(ys, axis=1)


def _s5_row_order(x, inverse=False):
    bn, seq, n = x.shape
    a, b = (S5_STEPS, S5_SEGS) if inverse else (S5_SEGS, S5_STEPS)
    return x.reshape(bn, seq // S5_CHUNK, a, b, n).transpose(0, 1, 3, 2, 4).reshape(bn, seq, n)


def _s5_scan(u, params, layer):
    b_blk, c_blk, lam1, pw = params
    bn, seq, _ = u.shape
    nc = seq // S5_CHUNK
    ns2 = 2 * S5_NSTATE
    u_blk = (1, S5_CHUNK, GROUP_WIDTH)
    return pl.pallas_call(
        _s5_scan_kernel,
        grid=(bn, nc),
        in_specs=[
            pl.BlockSpec(u_blk, lambda b, c: (b, c, 0)),
            pl.BlockSpec(u_blk, lambda b, c: (b, nc - 1 - c, 0)),
            pl.BlockSpec((None, 2, 2, S5_HALF_CH, S5_NSTATE), lambda b, c: (layer, 0, 0, 0, 0)),
            pl.BlockSpec((None, 2, 2, S5_NSTATE, S5_HALF_CH), lambda b, c: (layer, 0, 0, 0, 0)),
            pl.BlockSpec((None, 2, 1, ns2), lambda b, c: (layer, 0, 0, 0)),
            pl.BlockSpec((None, 2, ns2 // LANES, S5_STEPS, LANES), lambda b, c: (layer, 0, 0, 0, 0)),
        ],
        out_specs=[
            pl.BlockSpec(u_blk, lambda b, c: (b, c, 0)),
            pl.BlockSpec(u_blk, lambda b, c: (b, nc - 1 - c, 0)),
        ],
        out_shape=[jax.ShapeDtypeStruct((bn, seq, GROUP_WIDTH), F32)] * 2,
        scratch_shapes=[
            pltpu.VMEM((2, ns2 // LANES, S5_CHUNK, LANES), F32),
            pltpu.VMEM((2, 1, ns2), F32),
            pltpu.VMEM((2, S5_SEGS, ns2), F32),
        ],
        compiler_params=_cparams(("parallel", "arbitrary")),
        name="s5_scan",
    )(u, u, b_blk, c_blk, lam1, pw)


def _s5_tail_kernel(yf_ref, yb_ref, u_ref, d_ref, wg_ref, g_ref, o_ref):
    y = yf_ref[...] + yb_ref[...] + d_ref[...] * u_ref[...]
    y = jax.nn.gelu(y)
    gate = jnp.dot(y.astype(BF16), wg_ref[...], preferred_element_type=F32)
    o_ref[...] = _rms(y * jax.nn.sigmoid(gate), g_ref[...]).astype(BF16)


def _s5_tail(yf, yb, u, d_skip, w_glu, gain, *, tm=512):
    m = yf.shape[0]
    row = pl.BlockSpec((tm, GROUP_WIDTH), lambda i: (i, 0))
    vec = pl.BlockSpec((1, GROUP_WIDTH), lambda i: (0, 0))
    return pl.pallas_call(
        _s5_tail_kernel,
        grid=(m // tm,),
        in_specs=[row, row, row, vec,
                  pl.BlockSpec((GROUP_WIDTH, GROUP_WIDTH), lambda i: (0, 0)), vec],
        out_specs=row,
        out_shape=jax.ShapeDtypeStruct((m, GROUP_WIDTH), BF16),
        compiler_params=_cparams(("parallel",)),
        name="s5_tail",
    )(yf, yb, u, d_skip.reshape(1, GROUP_WIDTH), w_glu.astype(BF16), gain.reshape(1, GROUP_WIDTH))


def _rope(x, cc, ss):
    return x * cc + pltpu.roll(x, HEAD_DIM // 2, axis=1) * ss


def _attn_kernel(q_ref, k_ref, v_ref, qg_ref, kg_ref, ccq_ref, ssq_ref, cck_ref, ssk_ref,
                 o_ref, kp_ref, vp_ref):
    @pl.when(pl.program_id(2) == 0)
    def _():
        kn = _rms(k_ref[0], kg_ref[...])
        kp_ref[...] = _rope(kn, cck_ref[...], ssk_ref[...]).astype(BF16)
        vp_ref[...] = v_ref[0].astype(BF16)

    rep = ATT_HEADS // KV_HEADS
    outs = []
    for r in range(rep):
        q = q_ref[0, :, r * HEAD_DIM:(r + 1) * HEAD_DIM]
        qn = _rope(_rms(q, qg_ref[...]), ccq_ref[...], ssq_ref[...]) * (HEAD_DIM ** -0.5)
        s = lax.dot_general(qn.astype(BF16), kp_ref[...], (((1,), (1,)), ((), ())),
                            preferred_element_type=F32)
        p = jnp.exp(s - jnp.max(s, axis=-1, keepdims=True))
        inv = 1.0 / jnp.sum(p, axis=-1, keepdims=True)
        o = jnp.dot(p.astype(BF16), vp_ref[...], preferred_element_type=F32) * inv
        outs.append(o)
    o_ref[0] = jnp.concatenate(outs, axis=1)


def _rope_tables(seq):
    rows = seq // GRID_W
    row_id = np.repeat(np.arange(rows), GRID_W).astype(np.float64)
    col_id = np.tile(np.arange(GRID_W), rows).astype(np.float64)
    inv = ROPE_THETA ** (-np.arange(0, AXIS_ROT_DIM, 2, dtype=np.float64) / AXIS_ROT_DIM)
    ang = np.concatenate([row_id[:, None] * inv, col_id[:, None] * inv], axis=-1)
    cos, sin = np.cos(ang), np.sin(ang)
    cc = np.concatenate([cos, cos], axis=1).astype(np.float32)
    ss = np.concatenate([-sin, sin], axis=1).astype(np.float32)
    return jnp.asarray(cc), jnp.asarray(ss)


def _attention_mix(proj3, q_gain, k_gain, cc, ss, *, tq=256):
    bn, seq, _ = proj3.shape
    rep = ATT_HEADS // KV_HEADS
    qw = rep * HEAD_DIM
    q_col0 = (2 * GROUP_WIDTH) // qw
    k_col0 = (3 * GROUP_WIDTH) // HEAD_DIM
    v_col0 = k_col0 + KV_HEADS
    vec = pl.BlockSpec((1, HEAD_DIM), lambda b, g, i: (0, 0))
    return pl.pallas_call(
        _attn_kernel,
        grid=(bn, KV_HEADS, seq // tq),
        in_specs=[
            pl.BlockSpec((1, tq, qw), lambda b, g, i: (b, i, q_col0 + g)),
            pl.BlockSpec((1, seq, HEAD_DIM), lambda b, g, i: (b, 0, k_col0 + g)),
            pl.BlockSpec((1, seq, HEAD_DIM), lambda b, g, i: (b, 0, v_col0 + g)),
            vec, vec,
            pl.BlockSpec((tq, HEAD_DIM), lambda b, g, i: (i, 0)),
            pl.BlockSpec((tq, HEAD_DIM), lambda b, g, i: (i, 0)),
            pl.BlockSpec((seq, HEAD_DIM), lambda b, g, i: (0, 0)),
            pl.BlockSpec((seq, HEAD_DIM), lambda b, g, i: (0, 0)),
        ],
        out_specs=pl.BlockSpec((1, tq, qw), lambda b, g, i: (b, i, g)),
        out_shape=jax.ShapeDtypeStruct((bn, seq, GROUP_WIDTH), F32),
        scratch_shapes=[pltpu.VMEM((seq, HEAD_DIM), BF16), pltpu.VMEM((seq, HEAD_DIM), BF16)],
        compiler_params=_cparams(("parallel", "parallel", "arbitrary")),
        name="gqa_attention",
    )(proj3, proj3, proj3, q_gain.reshape(1, HEAD_DIM), k_gain.reshape(1, HEAD_DIM),
      cc, ss, cc, ss)


def _head_ones(width):
    r = lax.broadcasted_iota(jnp.int32, (width, width), 0) // RW_HEAD_DIM
    c = lax.broadcasted_iota(jnp.int32, (width, width), 1) // RW_HEAD_DIM
    return (r == c).astype(BF16)


def _head_sum2(x, ones):
    hi, lo = _split2(x)
    return (jnp.dot(hi, ones, preferred_element_type=F32)
            + jnp.dot(lo, ones, preferred_element_type=F32))


def _rwkv_prep_kernel(c_ref, p_ref, n_ref, mu_ref, vecs_ref, dvec_ref,
                      g1_ref, g2_ref, w1_ref, w2_ref, a1_ref, a2_ref,
                      r_o, v_o, nkk_o, g_o, bonus_o, w_o, kd_o, b_o, *, tiles_per_seq):
    i = pl.program_id(0)
    tm = c_ref.shape[0]
    first = (i % tiles_per_seq) == 0
    last = (i % tiles_per_seq) == tiles_per_seq - 1
    rows = lax.broadcasted_iota(jnp.int32, (tm, GROUP_WIDTH), 0)
    ones = _head_ones(GROUP_WIDTH)

    def shifted(j):
        sl = slice(j * GROUP_WIDTH, (j + 1) * GROUP_WIDTH)
        x = c_ref[:, sl]
        prev_row = jnp.where(first, 0.0, p_ref[7:8, sl])
        next_row = jnp.where(last, 0.0, n_ref[0:1, sl])
        prev = jnp.where(rows == 0, prev_row, pltpu.roll(x, 1, axis=0))
        nxt = jnp.where(rows == tm - 1, next_row, pltpu.roll(x, tm - 1, axis=0))
        return x + (prev - x) * mu_ref[j, 0:1, :] + (nxt - x) * mu_ref[j, 1:2, :]

    r, k, v, z = (shifted(j) for j in range(4))
    k_k, k_a, r_k = vecs_ref[0:1, :], vecs_ref[1:2, :], vecs_ref[2:3, :]
    zb = z.astype(BF16)

    gmid = jax.nn.sigmoid(jnp.dot(zb, g1_ref[...], preferred_element_type=F32))
    g_o[...] = jnp.dot(gmid.astype(BF16), g2_ref[...], preferred_element_type=F32)

    kk = k * k_k
    kk = kk * lax.rsqrt(_head_sum2(kk * kk, ones) + 1e-12)
    r_o[...] = r
    v_o[...] = v
    nkk_o[...] = -kk

    bonus = jnp.zeros_like(r)
    for d in range(2):
        w0, a0 = dvec_ref[d, 0:1, :], dvec_ref[d, 1:2, :]
        wmid = jnp.tanh(jnp.dot(zb, w1_ref[d], preferred_element_type=F32))
        wlin = w0 + jnp.dot(wmid.astype(BF16), w2_ref[d], preferred_element_type=F32)
        w_log = -jax.nn.softplus(-wlin) - 0.5
        w_o[d] = -jnp.exp(w_log)
        amid = jnp.dot(zb, a1_ref[d], preferred_element_type=F32)
        a = jax.nn.sigmoid(a0 + jnp.dot(amid.astype(BF16), a2_ref[d], preferred_element_type=F32))
        kd = k * (1.0 + (a - 1.0) * k_a)
        kd_o[d] = kd
        b_o[d] = kk * a
        bonus = bonus + _head_sum2(r * kd * r_k, ones) * v
    bonus_o[...] = bonus


def _pad_cols(w, n):
    return jnp.pad(w, [(0, 0)] * (w.ndim - 1) + [(0, n - w.shape[-1])])


def _pad_rows(w, n):
    return jnp.pad(w, [(0, 0)] * (w.ndim - 2) + [(0, n - w.shape[-2]), (0, 0)])


def _rwkv_prep(proj, seq, mu, w0, w1, w2, a0, a1, a2, g1, g2, k_k, k_a, r_k, *, tm=512):
    m = proj.shape[0]
    wide = 4 * GROUP_WIDTH
    tiles_per_seq = seq // tm
    nblk8 = m // 8
    vecs = jnp.stack([k_k, k_a, r_k.reshape(GROUP_WIDTH)]
                     + [jnp.zeros((GROUP_WIDTH,), F32)] * 5).astype(F32)
    dvec = jnp.stack([jnp.stack([w0[d], a0[d]] + [jnp.zeros((GROUP_WIDTH,), F32)] * 6)
                      for d in range(2)]).astype(F32)
    row = pl.BlockSpec((tm, GROUP_WIDTH), lambda i: (i, 0))
    row2 = pl.BlockSpec((2, tm, GROUP_WIDTH), lambda i: (0, i, 0))
    full = lambda a: pl.BlockSpec(a.shape, lambda i: (0,) * a.ndim)
    g1p = _pad_cols(g1, LORA_PAD).astype(BF16)
    g2p = _pad_rows(g2, LORA_PAD).astype(BF16)
    w1p = _pad_cols(w1, LORA_PAD).astype(BF16)
    w2p = _pad_rows(w2, LORA_PAD).astype(BF16)
    a1p = _pad_cols(a1, LORA_PAD).astype(BF16)
    a2p = _pad_rows(a2, LORA_PAD).astype(BF16)
    sds = jax.ShapeDtypeStruct
    return pl.pallas_call(
        functools.partial(_rwkv_prep_kernel, tiles_per_seq=tiles_per_seq),
        grid=(m // tm,),
        in_specs=[
            pl.BlockSpec((tm, wide), lambda i: (i, 1)),
            pl.BlockSpec((8, wide), lambda i: (jnp.maximum(i * (tm // 8) - 1, 0), 1)),
            pl.BlockSpec((8, wide), lambda i: (jnp.minimum((i + 1) * (tm // 8), nblk8 - 1), 1)),
            full(mu), full(vecs), full(dvec),
            full(g1p), full(g2p), full(w1p), full(w2p), full(a1p), full(a2p),
        ],
        out_specs=[row, row, row, row, row, row2, row2, row2],
        out_shape=[sds((m, GROUP_WIDTH), F32)] * 5 + [sds((2, m, GROUP_WIDTH), F32)] * 3,
        compiler_params=_cparams(("parallel",)),
        name="rwkv_prep",
    )(proj, proj, proj, mu, vecs, dvec, g1p, g2p, w1p, w2p, a1p, a2p)


RW_CHUNK = 64
RW_BATCH_PER_ITER = 4


def _bdot(a, b):
    return jnp.dot(a.astype(BF16), b.astype(BF16), preferred_element_type=F32)


def _bdot_nt(a, b):
    return lax.dot_general(a.astype(BF16), b.astype(BF16), (((1,), (1,)), ((), ())),
                           preferred_element_type=F32)


def _bdot_tn(a, b):
    return lax.dot_general(a.astype(BF16), b.astype(BF16), (((0,), (0,)), ((), ())),
                           preferred_element_type=F32)


def _rwkv_scan_kernel(rf, vf, nf, lf, kf, bf, rb, vb, nb, lb, kb, bb, yf_ref, yb_ref, st_ref,
                      *, bn):
    c = RW_CHUNK
    pw = 2 * RW_HEAD_DIM

    @pl.when(pl.program_id(0) == 0)
    def _():
        st_ref[...] = jnp.zeros_like(st_ref)

    row = lax.broadcasted_iota(jnp.int32, (c, pw), 0)
    col = lax.broadcasted_iota(jnp.int32, (c, pw), 1)
    first = col < RW_HEAD_DIM
    colh = col % RW_HEAD_DIM
    incl2 = {0: (colh <= row).astype(F32), 1: (colh >= row).astype(F32)}
    strict2 = {0: (colh < row).astype(F32), 1: (colh > row).astype(F32)}
    eye2 = (colh == row).astype(F32)
    trow = lax.broadcasted_iota(jnp.int32, (c, c), 0)
    tcol = lax.broadcasted_iota(jnp.int32, (c, c), 1)
    tri = {0: (tcol <= trow).astype(BF16), 1: (tcol >= trow).astype(BF16)}
    br = lax.broadcasted_iota(jnp.int32, (pw, pw), 0) // RW_HEAD_DIM
    bc = lax.broadcasted_iota(jnp.int32, (pw, pw), 1) // RW_HEAD_DIM
    same_head = br == bc

    def bd(x):
        return jnp.concatenate([jnp.where(first, x, 0.0), jnp.where(first, 0.0, x)], axis=0)

    npair = GROUP_WIDTH // pw
    cat = jnp.concatenate

    def body(it, carry):
        ch = []
        chain_inputs = [(it * RW_BATCH_PER_ITER + bi, d, refs)
                        for bi in range(RW_BATCH_PER_ITER)
                        for d, refs in ((0, (rf, vf, nf, lf, kf, bf)), (1, (rb, vb, nb, lb, kb, bb)))]
        for b, d, refs in chain_inputs:
            r_, v_, a_, lw, k_, b_ = (ref[b] for ref in refs)
            hi = lw.astype(BF16)
            r1 = lw - hi.astype(F32)
            mid = r1.astype(BF16)
            lo = (r1 - mid.astype(F32)).astype(BF16)
            cs = (jnp.dot(tri[d], hi, preferred_element_type=F32)
                  + jnp.dot(tri[d], mid, preferred_element_type=F32)
                  + jnp.dot(tri[d], lo, preferred_element_type=F32))
            tot = cs[c - 1:c] if d == 0 else cs[0:1]
            g_incl = jnp.exp(cs)
            inv = jnp.exp(-cs)
            gend = jnp.exp(tot - cs)
            gtot = jnp.exp(tot)
            at = a_ * jnp.exp(cs - lw)
            bt = b_ * inv
            kt = k_ * inv
            rt = r_ * g_incl
            bh = b_ * gend
            kh = k_ * gend
            for p in range(npair):
                sl = slice(p * pw, (p + 1) * pw)
                ch.append(dict(b=b, d=d, idx=(b * 2 + d) * npair + p, vm=v_[:, sl], at=at[:, sl],
                               bt=bt[:, sl], kt=kt[:, sl], rt=rt[:, sl], bh=bh[:, sl],
                               kh=kh[:, sl], gtot=gtot[:, sl]))
        for q in ch:
            o = _bdot_nt(cat([q['at'], q['rt']], axis=0),
                         cat([bd(q['bt']), bd(q['kt'])], axis=0))
            q['a_ab'] = o[:c, :pw] * strict2[q['d']]
            q['a_ak'] = o[:c, pw:] * strict2[q['d']]
            q['q_bk'] = cat([o[c:, :pw] * incl2[q['d']], o[c:, pw:] * incl2[q['d']]], axis=1)
            q['tm'] = eye2 + q['a_ab']
        for q in ch:
            q['pp'] = _bdot(q['a_ab'], bd(q['a_ab']))
        for i in range(5):
            for q in ch:
                rr = _bdot(cat([q['tm'], q['pp']], axis=0), bd(q['pp']))
                q['tm'] = q['tm'] + rr[:c]
                q['pp'] = rr[c:]
        for q in ch:
            q['akv'] = _bdot(q['a_ak'], bd(q['vm']))
        for q in ch:
            q['w_uv'] = _bdot(q['tm'], cat([bd(q['at']), bd(q['akv'])], axis=1))
        for q in ch:
            q['s0'] = st_ref[q['idx']]
            q['uy'] = _bdot_nt(cat([q['w_uv'][:, :pw], q['rt']], axis=0), q['s0'])
        for q in ch:
            q['u'] = q['uy'][:c] + q['w_uv'][:, pw:]
            upd = _bdot_tn(cat([q['u'], q['vm']], axis=0), cat([q['bh'], q['kh']], axis=0))
            st_ref[q['idx']] = q['s0'] * q['gtot'] + jnp.where(same_head, upd, 0.0)
        for q in ch:
            q['y'] = q['uy'][c:] + _bdot(q['q_bk'], cat([bd(q['u']), bd(q['vm'])], axis=0))
        for k, (b, d, _) in enumerate(chain_inputs):
            y_ref = yf_ref if d == 0 else yb_ref
            y_ref[b] = cat([q['y'] for q in ch[k * npair:(k + 1) * npair]], axis=1)
        return carry

    lax.fori_loop(0, bn // RW_BATCH_PER_ITER, body, 0)


def _rwkv_scan(r, v, nkk, w, kd, b):
    bn, seq, _ = r.shape
    nc = seq // RW_CHUNK
    blk = (bn, RW_CHUNK, GROUP_WIDTH)
    fwd = pl.BlockSpec(blk, lambda c: (0, c, 0))
    bwd = pl.BlockSpec(blk, lambda c: (0, nc - 1 - c, 0))
    fwd_d = pl.BlockSpec((None,) + blk, lambda c: (0, 0, c, 0))
    bwd_d = pl.BlockSpec((None,) + blk, lambda c: (1, 0, nc - 1 - c, 0))
    return pl.pallas_call(
        functools.partial(_rwkv_scan_kernel, bn=bn),
        grid=(nc,),
        in_specs=[fwd, fwd, fwd, fwd_d, fwd_d, fwd_d, bwd, bwd, bwd, bwd_d, bwd_d, bwd_d],
        out_specs=[fwd, bwd],
        out_shape=[jax.ShapeDtypeStruct((bn, seq, GROUP_WIDTH), F32)] * 2,
        scratch_shapes=[pltpu.VMEM((2 * bn * (RW_HEADS // 2), 2 * RW_HEAD_DIM, 2 * RW_HEAD_DIM), F32)],
        compiler_params=_cparams(("arbitrary",)),
        name="rwkv_scan",
    )(r, v, nkk, w, kd, b, r, v, nkk, w, kd, b)


def _rwkv_post_kernel(yf_ref, yb_ref, bonus_ref, g_ref, lnw_ref, lnb_ref, gain_ref, o_ref):
    ones = _head_ones(GROUP_WIDTH)
    y = yf_ref[...] + yb_ref[...]
    mean = _head_sum2(y, ones) * (1.0 / RW_HEAD_DIM)
    yc = y - mean
    var = _head_sum2(yc * yc, ones) * (1.0 / RW_HEAD_DIM)
    yn = yc * lax.rsqrt(var + RW_GN_EPS) * lnw_ref[...] + lnb_ref[...]
    out = (yn + bonus_ref[...]) * g_ref[...]
    o_ref[...] = _rms(out, gain_ref[...]).astype(BF16)


def _rwkv_post(yf, yb, bonus, g, ln_w, ln_b, gain, *, tm=512):
    m = yf.shape[0]
    row = pl.BlockSpec((tm, GROUP_WIDTH), lambda i: (i, 0))
    vec = pl.BlockSpec((1, GROUP_WIDTH), lambda i: (0, 0))
    v2 = lambda a: a.reshape(1, GROUP_WIDTH)
    return pl.pallas_call(
        _rwkv_post_kernel,
        grid=(m // tm,),
        in_specs=[row, row, row, row, vec, vec, vec],
        out_specs=row,
        out_shape=jax.ShapeDtypeStruct((m, GROUP_WIDTH), BF16),
        compiler_params=_cparams(("parallel",)),
        name="rwkv_post",
    )(yf, yb, bonus, g, v2(ln_w), v2(ln_b), v2(gain))


def _norm_rows_kernel(x_ref, g_ref, o_ref):
    o_ref[...] = _rms(x_ref[...], g_ref[...]).astype(BF16)


def _norm_rows(x, gain, *, tm=512):
    m, n = x.shape
    return pl.pallas_call(
        _norm_rows_kernel,
        grid=(m // tm,),
        in_specs=[pl.BlockSpec((tm, n), lambda i: (i, 0)), pl.BlockSpec((1, n), lambda i: (0, 0))],
        out_specs=pl.BlockSpec((tm, n), lambda i: (i, 0)),
        out_shape=jax.ShapeDtypeStruct((m, n), BF16),
        compiler_params=_cparams(("parallel",)),
        name="norm_rows",
    )(x, gain.reshape(1, n))


def _out_proj_kernel(ya_ref, yb_ref, yc_ref, yd_ref, w_ref, x_ref, o_ref):
    acc = x_ref[...]
    for g, y_ref in enumerate((ya_ref, yb_ref, yc_ref, yd_ref)):
        acc = acc + jnp.dot(y_ref[...], w_ref[g * GROUP_WIDTH:(g + 1) * GROUP_WIDTH, :],
                            preferred_element_type=F32)
    o_ref[...] = acc


def _out_proj(ya, yb, yc, yd, w_out, x, *, tm=1024, tn=1024):
    m, n = x.shape
    row = pl.BlockSpec((tm, GROUP_WIDTH), lambda i, j: (i, 0))
    return pl.pallas_call(
        _out_proj_kernel,
        grid=(m // tm, n // tn),
        in_specs=[row, row, row, row,
                  pl.BlockSpec((N_MIXERS * GROUP_WIDTH, tn), lambda i, j: (0, j)),
                  pl.BlockSpec((tm, tn), lambda i, j: (i, j))],
        out_specs=pl.BlockSpec((tm, tn), lambda i, j: (i, j)),
        out_shape=jax.ShapeDtypeStruct((m, n), F32),
        compiler_params=_cparams(("parallel", "parallel")),
        name="out_proj",
    )(ya, yb, yc, yd, w_out.astype(BF16), x)


PEER_TOK = 256
NEG_INF = float("-inf")


def _peer_stats_kernel(q_ref, khi_ref, klo_ref, tau_o, e1_o, e2_o,
                       sc1_ref, sc2_ref, top_ref, cand_ref):
    nt = q_ref.shape[0]
    for h in range(PEER_HEADS):
        qh = q_ref[:, h * LANES:(h + 1) * LANES]
        qhi, qlo = _split2(qh)
        nt_dot = lambda a, b: lax.dot_general(a, b, (((1,), (1,)), ((), ())),
                                              preferred_element_type=F32)
        for p in range(2):
            sc = (nt_dot(khi_ref[h, p], qhi) + nt_dot(khi_ref[h, p], qlo)
                  + nt_dot(klo_ref[h, p], qhi))
            if p == 0:
                sc1_ref[...] = sc
            else:
                sc2_ref[...] = sc
            x = sc
            for i in range(PEER_TOPK):
                m = jnp.max(x, axis=0, keepdims=True)
                top_ref[p, i:i + 1, :] = m
                x = jnp.where(x == m, NEG_INF, x)
        s2 = top_ref[1]
        sub = lax.broadcasted_iota(jnp.int32, (8, nt), 0)
        cand_ref[0:16, :] = top_ref[0, 0:1, :] + s2
        cand_ref[16:24, :] = top_ref[0, 1:2, :] + s2[0:8]
        for g, (i, nj) in enumerate(((2, 5), (3, 4), (4, 3), (5, 2), (6, 2), (7, 2))):
            cand_ref[24 + 8 * g:32 + 8 * g, :] = jnp.where(sub < nj, top_ref[0, i:i + 1, :] + s2[0:8],
                                                          NEG_INF)
        cand_ref[72:80, :] = top_ref[0, 8:16, :] + s2[0:1]
        x = cand_ref[...]
        top = top_ref[0, 0:1, :] + top_ref[1, 0:1, :]
        zsum = jnp.zeros((1, nt), F32)
        m = top
        for i in range(PEER_TOPK):
            m = jnp.max(x, axis=0, keepdims=True)
            zsum = zsum + jnp.exp(m - top)
            x = jnp.where(x == m, NEG_INF, x)
        thr = m
        sc1 = sc1_ref[...]
        th = jnp.full(sc1.shape, jnp.inf, F32)
        for i in range(PEER_TOPK):
            s1i = top_ref[0, i:i + 1, :]
            row_th = jnp.min(jnp.where(s1i + s2 >= thr, s2, jnp.inf), axis=0, keepdims=True)
            th = jnp.where(sc1 == s1i, row_th, th)
        e1 = jnp.exp(sc1 - top_ref[0, 0:1, :]) / zsum
        e2 = jnp.exp(sc2_ref[...] - top_ref[1, 0:1, :])
        tau = jnp.exp(th - top_ref[1, 0:1, :])
        for t in range(nt // LANES):
            tl = slice(t * LANES, (t + 1) * LANES)
            tau_o[h, t] = tau[:, tl]
            e1_o[h, t] = e1[:, tl]
            e2_o[h, t] = e2[:, tl]


def _peer_stats(q, sub_keys):
    ntok = q.shape[0]
    keys = jnp.zeros((PEER_HEADS, 2, PEER_N_KEYS, LANES), F32)
    keys = keys.at[:, 0, :, :PEER_HALF].set(sub_keys[:, 0]).at[:, 1, :, PEER_HALF:].set(sub_keys[:, 1])
    khi = keys.astype(BF16)
    klo = (keys - khi.astype(F32)).astype(BF16)
    kspec = pl.BlockSpec(keys.shape, lambda i: (0, 0, 0, 0))
    big = pl.BlockSpec((PEER_HEADS, PEER_TOK // LANES, PEER_N_KEYS, LANES), lambda i: (0, i, 0, 0))
    return pl.pallas_call(
        _peer_stats_kernel,
        grid=(ntok // PEER_TOK,),
        in_specs=[pl.BlockSpec((PEER_TOK, PEER_HEADS * LANES), lambda i: (i, 0)), kspec, kspec],
        out_specs=[big] * 3,
        out_shape=[jax.ShapeDtypeStruct((PEER_HEADS, ntok // LANES, PEER_N_KEYS, LANES), F32)] * 3,
        scratch_shapes=[pltpu.VMEM((PEER_N_KEYS, PEER_TOK), F32),
                        pltpu.VMEM((PEER_N_KEYS, PEER_TOK), F32),
                        pltpu.VMEM((2, PEER_TOPK, PEER_TOK), F32),
                        pltpu.VMEM((80, PEER_TOK), F32)],
        compiler_params=_cparams(("parallel",)),
        name="peer_stats",
    )(q, khi, klo)


PEER_TB = 512
PEER_ET = 1024


PEER_TC = 256


def _peer_act_kernel(hnt_ref, u_ref, g_ref):
    act = jnp.dot(u_ref[...], hnt_ref[...], preferred_element_type=F32)
    g_ref[...] = (0.5 * act * (1.0 + lax.erf(act * (1.0 / math.sqrt(2.0))))).astype(BF16)


def _peer_act(hnt, u_all, layer):
    d, ntok = hnt.shape
    nexp = u_all.shape[1]
    return pl.pallas_call(
        _peer_act_kernel,
        grid=(ntok // PEER_TB, nexp // PEER_ET),
        in_specs=[pl.BlockSpec((d, PEER_TB), lambda i, j: (0, i)),
                  pl.BlockSpec((None, PEER_ET, d), lambda i, j: (layer, j, 0))],
        out_specs=pl.BlockSpec((None, PEER_ET, PEER_TB), lambda i, j: (i, j, 0)),
        out_shape=jax.ShapeDtypeStruct((ntok // PEER_TB, nexp, PEER_TB), BF16),
        compiler_params=_cparams(("parallel", "parallel")),
        name="peer_act",
    )(hnt, u_all)


def _peer_mix_kernel(vt_ref, g_ref, tau_ref, e1_ref, e2_ref, x_ref, o_ref, acc_ref, w_ref):
    j = pl.program_id(1)
    a0 = j * (PEER_ET // PEER_N_KEYS)

    @pl.when(j == 0)
    def _():
        acc_ref[...] = jnp.zeros_like(acc_ref)

    for c in range(PEER_TB // PEER_TC):
        cl = slice(c * PEER_TC, (c + 1) * PEER_TC)
        for t in range(c * PEER_TC // LANES, (c + 1) * PEER_TC // LANES):
            tl = slice(t * LANES, (t + 1) * LANES)
            for ai in range(PEER_ET // PEER_N_KEYS):
                gate = jnp.zeros((PEER_N_KEYS, LANES), F32)
                for h in range(PEER_HEADS):
                    tau = tau_ref[h, t, pl.ds(a0 + ai, 1), :]
                    e1 = e1_ref[h, t, pl.ds(a0 + ai, 1), :]
                    e2 = e2_ref[h, t]
                    gate = gate + jnp.where(e2 >= tau, e2, 0.0) * e1
                rows = slice(ai * PEER_N_KEYS, (ai + 1) * PEER_N_KEYS)
                w_ref[rows, tl] = (gate * g_ref[rows, tl].astype(F32)).astype(BF16)
        acc_ref[:, cl] += jnp.dot(vt_ref[...], w_ref[:, cl], preferred_element_type=F32)

    @pl.when(j == pl.num_programs(1) - 1)
    def _():
        o_ref[...] = x_ref[...] + acc_ref[...].T


def _peer_mix(vt_all, layer, g, tau, e1, e2, x):
    ntok, d = x.shape
    nexp = vt_all.shape[1] * PEER_ET
    stat = pl.BlockSpec((PEER_HEADS, PEER_TB // LANES, PEER_N_KEYS, LANES), lambda i, j: (0, i, 0, 0))
    tok = pl.BlockSpec((PEER_TB, d), lambda i, j: (i, 0))
    return pl.pallas_call(
        _peer_mix_kernel,
        grid=(ntok // PEER_TB, nexp // PEER_ET),
        in_specs=[pl.BlockSpec((None, None, d, PEER_ET), lambda i, j: (layer, j, 0, 0)),
                  pl.BlockSpec((None, PEER_ET, PEER_TB), lambda i, j: (i, j, 0)),
                  stat, stat, stat, tok],
        out_specs=tok,
        out_shape=jax.ShapeDtypeStruct((ntok, d), F32),
        scratch_shapes=[pltpu.VMEM((d, PEER_TB), F32), pltpu.VMEM((PEER_ET, PEER_TB), BF16)],
        compiler_params=_cparams(("parallel", "arbitrary")),
        name="peer_mix",
    )(vt_all, g, tau, e1, e2, x)


def _peer_vt(peer_v):
    depth, nexp, d = peer_v.shape
    return peer_v.astype(BF16).reshape(depth, nexp // PEER_ET, PEER_ET, d).transpose(0, 1, 3, 2)


def _peer_ffn(x, norm_gain, w_query, sub_keys, u_all, vt_all, layer):
    q, hnt = _norm_matmul(x, norm_gain, w_query.astype(BF16), emit_xn_t=True)
    tau, e1, e2 = _peer_stats(q, sub_keys)
    g = _peer_act(hnt, u_all, layer)
    return _peer_mix(vt_all, layer, g, tau, e1, e2, x)


def kernel(x, norm_mix, w_in, fourier_w, s5_lam_re, s5_lam_im, s5_log_step, s5_b_re, s5_b_im, s5_c_re, s5_c_im, s5_d, s5_w_glu, attn_q_norm, attn_k_norm, rwkv_mu, rwkv_w0, rwkv_w1, rwkv_w2, rwkv_a0, rwkv_a1, rwkv_a2, rwkv_g1, rwkv_g2, rwkv_k_k, rwkv_k_a, rwkv_r_k, rwkv_ln_w, rwkv_ln_b, branch_norm, w_out, norm_ffn, peer_w_query, peer_sub_keys, peer_u, peer_v):
    bn, seq, d = x.shape
    m = bn * seq
    depth = w_in.shape[0]
    dft_s, dft_c = _dft_tables(seq)
    cc, ss = _rope_tables(seq)
    u_all = peer_u.astype(BF16)
    vt_all = _peer_vt(peer_v)
    s5_all = jax.vmap(_s5_params)(s5_lam_re, s5_lam_im, s5_log_step, s5_b_re, s5_b_im, s5_c_re, s5_c_im)
    xf = x.reshape(m, d)
    for l in range(depth):
        bgain = branch_norm[l].reshape(N_MIXERS, GROUP_WIDTH)
        proj = _norm_matmul(xf, norm_mix[l], w_in[l].astype(BF16), tm=1024)
        proj3 = proj.reshape(bn, seq, PROJ_WIDTH)

        ya = _fnet_mix(proj3, dft_s, dft_c, fourier_w[l], bgain[0]).reshape(m, GROUP_WIDTH)

        u_s5 = _s5_row_order(proj3[:, :, GROUP_WIDTH:2 * GROUP_WIDTH])
        yf, yb = _s5_scan(u_s5, s5_all, l)
        ybm = _s5_tail(yf.reshape(m, GROUP_WIDTH), yb.reshape(m, GROUP_WIDTH),
                       u_s5.reshape(m, GROUP_WIDTH), s5_d[l], s5_w_glu[l], bgain[1])
        ybm = _s5_row_order(ybm.reshape(bn, seq, GROUP_WIDTH), inverse=True).reshape(m, GROUP_WIDTH)

        att = _attention_mix(proj3, attn_q_norm[l], attn_k_norm[l], cc, ss)
        yc = _norm_rows(att.reshape(m, GROUP_WIDTH), bgain[2])

        r, v, nkk, g, bonus, w, kd, b = _rwkv_prep(
            proj, seq, rwkv_mu[l], rwkv_w0[l], rwkv_w1[l], rwkv_w2[l], rwkv_a0[l], rwkv_a1[l],
            rwkv_a2[l], rwkv_g1[l], rwkv_g2[l], rwkv_k_k[l], rwkv_k_a[l], rwkv_r_k[l])
        s3 = lambda a: a.reshape(bn, seq, GROUP_WIDTH)
        s4 = lambda a: a.reshape(2, bn, seq, GROUP_WIDTH)
        y_f, y_b = _rwkv_scan(s3(r), s3(v), s3(nkk), s4(w), s4(kd), s4(b))
        yd = _rwkv_post(y_f.reshape(m, GROUP_WIDTH), y_b.reshape(m, GROUP_WIDTH), bonus, g,
                        rwkv_ln_w[l], rwkv_ln_b[l], bgain[3])

        xf = _out_proj(ya, ybm, yc, yd, w_out[l], xf)
        xf = _peer_ffn(xf, norm_ffn[l], peer_w_query[l], peer_sub_keys[l], u_all, vt_all, l)
    return xf.reshape(bn, seq, d)
```
